```python
import jax, jax.numpy as jnp
from jax import lax
import numpy as np

D_MODEL = 1024
BATCH = 32
SEQ = 2048
DEPTH = 1

N_META = 16
GRID_W = 64
Q_BLOCK = 128
ROPE_THETA = 10000.0
RMS_EPS = 1e-6
LN_EPS = 1e-5
D_FF = 2816

A_HEADS = 8
A_KV_HEADS = 2
A_HEAD_DIM = 64
A_GROUP = A_HEADS // A_KV_HEADS
B_HEADS = 8
B_Q_RANK = 256
B_KV_RANK = 128
B_NOPE_DIM = 64
B_ROPE_DIM = 32
B_V_DIM = 64
B_QK_DIM = B_NOPE_DIM + B_ROPE_DIM

A_WIDTH = A_HEADS * A_HEAD_DIM
B_WIDTH = B_HEADS * B_V_DIM
MIX_WIDTH = A_WIDTH + B_WIDTH
IN_SPLITS = (A_HEADS * A_HEAD_DIM, A_KV_HEADS * A_HEAD_DIM, A_KV_HEADS * A_HEAD_DIM, B_Q_RANK, B_KV_RANK, B_ROPE_DIM)
IN_WIDTH = sum(IN_SPLITS)
IN_OFFSETS = [int(o) for o in np.cumsum(IN_SPLITS)[:-1]]

DEEPNORM_ALPHA = (2.0 * DEPTH) ** 0.25
DEEPNORM_BETA = (8.0 * DEPTH) ** -0.25

kernel_name = "hybrid_gqa_mla_macaron_deepnorm_encoder"


def rms_norm(x, g):
    xf = x.astype(jnp.float32)
    y = xf * lax.rsqrt(jnp.mean(xf * xf, axis=-1, keepdims=True) + RMS_EPS)
    return (y * g.astype(jnp.float32)).astype(x.dtype)


def layer_norm(x, g, b):
    xf = x.astype(jnp.float32)
    mu = jnp.mean(xf, axis=-1, keepdims=True)
    var = jnp.mean(jnp.square(xf - mu), axis=-1, keepdims=True)
    y = (xf - mu) * lax.rsqrt(var + LN_EPS)
    return (y * g.astype(jnp.float32) + b.astype(jnp.float32)).astype(x.dtype)


def swiglu(x, w1, w3, w2):
    return (jax.nn.silu(x @ w1) * (x @ w3)) @ w2


def axial_rope_tables(n_tokens, rot_dim, dtype):
    rows_count = n_tokens // GRID_W
    rows = jnp.repeat(jnp.arange(rows_count, dtype=jnp.int32), GRID_W)
    cols = jnp.tile(jnp.arange(GRID_W, dtype=jnp.int32), rows_count)
    axis_dim = rot_dim // 2
    inv_freq = ROPE_THETA ** (-jnp.arange(0, axis_dim, 2, dtype=jnp.float32) / axis_dim)
    ang = jnp.concatenate([rows.astype(jnp.float32)[:, None] * inv_freq,
                           cols.astype(jnp.float32)[:, None] * inv_freq], axis=-1)
    ang = jnp.concatenate([jnp.zeros((N_META, rot_dim // 2), jnp.float32), ang], axis=0)
    return jnp.cos(ang).astype(dtype), jnp.sin(ang).astype(dtype)


def apply_rope(x, cos, sin):
    half = x.shape[-1] // 2
    x1, x2 = x[..., :half], x[..., half:]
    c, s = cos[None, :, None, :], sin[None, :, None, :]
    return jnp.concatenate([x1 * c - x2 * s, x1 * s + x2 * c], axis=-1)


def block_attention(q, k, v, scale):
    def attend(qb):
        s = jnp.einsum("bqhgd,bkhd->bhgqk", qb, k).astype(jnp.float32) * scale
        p = jax.nn.softmax(s, axis=-1).astype(v.dtype)
        return jnp.einsum("bhgqk,bkhd->bqhgd", p, v)

    bsz, length, hkv, grp, dk = q.shape
    n_real = length - N_META
    o_meta = attend(q[:, :N_META])
    q_real = q[:, N_META:].reshape(bsz, n_real // Q_BLOCK, Q_BLOCK, hkv, grp, dk)
    o_real = lax.map(attend, jnp.moveaxis(q_real, 1, 0))
    o_real = jnp.moveaxis(o_real, 0, 1).reshape(bsz, n_real, hkv, grp, v.shape[-1])
    return jnp.concatenate([o_meta, o_real], axis=1)


def parallel_mixer(h, cos_a, sin_a, cos_b, sin_b, w_in, q_norm_a, k_norm_a, cq_norm, ckv_norm,
                   w_uq, w_ukv, out_norm_a, out_norm_b, w_out):
    bsz, length, _ = h.shape
    proj = h @ w_in
    q_a, k_a, v_a, c_q, c_kv, k_pe = jnp.split(proj, IN_OFFSETS, axis=-1)

    q_a = apply_rope(rms_norm(q_a.reshape(bsz, length, A_HEADS, A_HEAD_DIM), q_norm_a), cos_a, sin_a)
    k_a = apply_rope(rms_norm(k_a.reshape(bsz, length, A_KV_HEADS, A_HEAD_DIM), k_norm_a), cos_a, sin_a)
    v_a = v_a.reshape(bsz, length, A_KV_HEADS, A_HEAD_DIM)
    q_a = q_a.reshape(bsz, length, A_KV_HEADS, A_GROUP, A_HEAD_DIM)
    o_a = block_attention(q_a, k_a, v_a, A_HEAD_DIM ** -0.5).reshape(bsz, length, A_WIDTH)

    q_b = (rms_norm(c_q, cq_norm) @ w_uq).reshape(bsz, length, B_HEADS, B_QK_DIM)
    q_nope, q_pe = q_b[..., :B_NOPE_DIM], q_b[..., B_NOPE_DIM:]
    q_b = jnp.concatenate([q_nope, apply_rope(q_pe, cos_b, sin_b)], axis=-1)
    kv_b = (rms_norm(c_kv, ckv_norm) @ w_ukv).reshape(bsz, length, B_HEADS, B_NOPE_DIM + B_V_DIM)
    k_nope, v_b = kv_b[..., :B_NOPE_DIM], kv_b[..., B_NOPE_DIM:]
    k_pe = apply_rope(k_pe[:, :, None, :], cos_b, sin_b)
    k_b = jnp.concatenate([k_nope, jnp.broadcast_to(k_pe, (bsz, length, B_HEADS, B_ROPE_DIM))], axis=-1)
    q_b = q_b.reshape(bsz, length, B_HEADS, 1, B_QK_DIM)
    o_b = block_attention(q_b, k_b, v_b, B_QK_DIM ** -0.5).reshape(bsz, length, B_WIDTH)

    o = jnp.concatenate([rms_norm(o_a, out_norm_a), rms_norm(o_b, out_norm_b)], axis=-1)
    return o @ w_out


def _normal(k, shape, scale):
    return jax.random.normal(k, shape, jnp.float32) * scale


def setup_inputs(seed: int = 0) -> dict:
    key = jax.random.key(seed)
    ks = jax.random.split(key, 24)
    D, L = D_MODEL, DEPTH
    gain = lambda k, n: 1.0 + _normal(k, (L, n), 0.02)
    bias = lambda k, n: _normal(k, (L, n), 0.02)
    return {
        "x": _normal(ks[0], (BATCH, SEQ, D), 1.0),
        "meta_tokens": _normal(ks[1], (N_META, D), 1.0),
        "ffn1_w1": _normal(ks[2], (L, D, D_FF), D ** -0.5),
        "ffn1_w3": _normal(ks[3], (L, D, D_FF), D ** -0.5),
        "ffn1_w2": _normal(ks[4], (L, D_FF, D), DEEPNORM_BETA * D_FF ** -0.5),
        "ln1_g": gain(ks[5], D),
        "ln1_b": bias(ks[6], D),
        "w_in": _normal(ks[7], (L, D, IN_WIDTH), D ** -0.5),
        "q_norm_a": gain(ks[8], A_HEAD_DIM),
        "k_norm_a": gain(ks[9], A_HEAD_DIM),
        "cq_norm": gain(ks[10], B_Q_RANK),
        "ckv_norm": gain(ks[11], B_KV_RANK),
        "w_uq": _normal(ks[12], (L, B_Q_RANK, B_HEADS * B_QK_DIM), B_Q_RANK ** -0.5),
        "w_ukv": _normal(ks[13], (L, B_KV_RANK, B_HEADS * (B_NOPE_DIM + B_V_DIM)), B_KV_RANK ** -0.5),
        "out_norm_a": gain(ks[14], A_WIDTH),
        "out_norm_b": gain(ks[15], B_WIDTH),
        "w_out": _normal(ks[16], (L, MIX_WIDTH, D), DEEPNORM_BETA * MIX_WIDTH ** -0.5),
        "ln2_g": gain(ks[17], D),
        "ln2_b": bias(ks[18], D),
        "ffn2_w1": _normal(ks[19], (L, D, D_FF), D ** -0.5),
        "ffn2_w3": _normal(ks[20], (L, D, D_FF), D ** -0.5),
        "ffn2_w2": _normal(ks[21], (L, D_FF, D), DEEPNORM_BETA * D_FF ** -0.5),
        "ln3_g": gain(ks[22], D),
        "ln3_b": bias(ks[23], D),
    }


def reference(x, meta_tokens, ffn1_w1, ffn1_w3, ffn1_w2, ln1_g, ln1_b, w_in, q_norm_a, k_norm_a,
              cq_norm, ckv_norm, w_uq, w_ukv, out_norm_a, out_norm_b, w_out, ln2_g, ln2_b,
              ffn2_w1, ffn2_w3, ffn2_w2, ln3_g, ln3_b):
    bsz, n_tok, d = x.shape
    meta = jnp.broadcast_to(meta_tokens[None].astype(x.dtype), (bsz, N_META, d))
    h = jnp.concatenate([meta, x], axis=1)
    cos_a, sin_a = axial_rope_tables(n_tok, A_HEAD_DIM, x.dtype)
    cos_b, sin_b = axial_rope_tables(n_tok, B_ROPE_DIM, x.dtype)
    alpha = DEEPNORM_ALPHA
    for i in range(DEPTH):
        h = layer_norm(alpha * h + 0.5 * swiglu(h, ffn1_w1[i], ffn1_w3[i], ffn1_w2[i]), ln1_g[i], ln1_b[i])
        mix = parallel_mixer(h, cos_a, sin_a, cos_b, sin_b, w_in[i], q_norm_a[i], k_norm_a[i],
                             cq_norm[i], ckv_norm[i], w_uq[i], w_ukv[i], out_norm_a[i], out_norm_b[i], w_out[i])
        h = layer_norm(alpha * h + mix, ln2_g[i], ln2_b[i])
        h = layer_norm(alpha * h + 0.5 * swiglu(h, ffn2_w1[i], ffn2_w3[i], ffn2_w2[i]), ln3_g[i], ln3_b[i])
    return h[:, N_META:]
```

```python
import functools

import jax
import jax.numpy as jnp
import numpy as np
from jax import lax
from jax.experimental import pallas as pl
from jax.experimental.pallas import tpu as pltpu

F32 = jnp.float32
BF16 = jnp.bfloat16

LANES = 128
VMEM_LIMIT_BYTES = 56 * 1024 * 1024

N_META = 16
GRID_W = 64
ROPE_THETA = 10000.0
RMS_EPS = 1e-6
LN_EPS = 1e-5
A_HEADS, A_KV_HEADS, A_HEAD_DIM = 8, 2, 64
A_GROUP = A_HEADS // A_KV_HEADS
B_HEADS, B_Q_RANK, B_KV_RANK = 8, 256, 128
B_NOPE_DIM, B_ROPE_DIM, B_V_DIM = 64, 32, 64
B_QK_DIM = B_NOPE_DIM + B_ROPE_DIM
A_WIDTH = A_HEADS * A_HEAD_DIM
B_WIDTH = B_HEADS * B_V_DIM
DEPTH = 1
ALPHA = (2.0 * DEPTH) ** 0.25

OFF_QA, OFF_KA, OFF_VA = 0, A_WIDTH, A_WIDTH + LANES
OFF_CQ = OFF_VA + LANES
OFF_CKV = OFF_CQ + B_Q_RANK
OFF_KPE = OFF_CKV + B_KV_RANK
IN_PAD_WIDTH = OFF_KPE + LANES

FFN_CHUNK = 256
ROW_TILE = 512
Q_TILE = 256


def _layer_norm(z, g, b):
    mu = jnp.mean(z, axis=-1, keepdims=True)
    zc = z - mu
    var = jnp.mean(zc * zc, axis=-1, keepdims=True)
    return zc * lax.rsqrt(var + LN_EPS) * g + b


def _swiglu(xb, w1_ref, w3_ref, w2_ref, acc_ref):
    acc_ref[...] = jnp.zeros_like(acc_ref)

    def body(c, carry):
        u = jnp.dot(xb, w1_ref[c], preferred_element_type=F32)
        g = jnp.dot(xb, w3_ref[c], preferred_element_type=F32)
        a = (u * jax.nn.sigmoid(u) * g).astype(BF16)
        acc_ref[...] += jnp.dot(a, w2_ref[c], preferred_element_type=F32)
        return carry

    lax.fori_loop(0, w1_ref.shape[0], body, 0)
    return acc_ref[...]


def _rot_a(n, lane):
    return jnp.where((lane % A_HEAD_DIM) < A_HEAD_DIM // 2,
                     pltpu.roll(n, LANES - A_HEAD_DIM // 2, 1), pltpu.roll(n, A_HEAD_DIM // 2, 1))


def _rot_b(x, lane):
    half = B_ROPE_DIM // 2
    return jnp.where(lane < B_NOPE_DIM + half, pltpu.roll(x, LANES - half, 1), pltpu.roll(x, half, 1))


def _ffn_proj_kernel(x_ref, w1_ref, w3_ref, w2_ref, lng_ref, lnb_ref, win_ref,
                     gq_ref, gk_ref, gcq_ref, gckv_ref, wuq_ref, wuk_ref, wuv_ref,
                     ca_ref, sa_ref, cb_ref, sb_ref,
                     h1_ref, qa_ref, ka_ref, va_ref, qb_ref, kb_ref, vb_ref, acc_ref):
    tm = x_ref.shape[0]
    x = x_ref[...]
    y = _swiglu(x.astype(BF16), w1_ref, w3_ref, w2_ref, acc_ref)
    h1 = _layer_norm(ALPHA * x + 0.5 * y, lng_ref[...], lnb_ref[...])
    h1_ref[...] = h1

    proj = jnp.dot(h1.astype(BF16), win_ref[...], preferred_element_type=F32)

    n_pos_blocks = ca_ref.shape[0] // tm
    row0 = pl.multiple_of((pl.program_id(0) % n_pos_blocks) * tm, tm)
    ca = ca_ref[pl.ds(row0, tm), :]
    sa = sa_ref[pl.ds(row0, tm), :]
    cb = cb_ref[pl.ds(row0, tm), :]
    sb = sb_ref[pl.ds(row0, tm), :]

    lane = lax.broadcasted_iota(jnp.int32, (tm, LANES), 1)
    lo = lane < A_HEAD_DIM

    def norm_rope_a(xc, g):
        x2 = xc * xc
        s_lo = jnp.sum(jnp.where(lo, x2, 0.0), axis=-1, keepdims=True)
        s_hi = jnp.sum(jnp.where(lo, 0.0, x2), axis=-1, keepdims=True)
        ms = jnp.where(lo, s_lo, s_hi) * (1.0 / A_HEAD_DIM)
        n = xc * lax.rsqrt(ms + RMS_EPS) * g
        return n * ca + _rot_a(n, lane) * sa

    gq = gq_ref[...]
    for c in range(A_WIDTH // LANES):
        qa_ref[:, c * LANES:(c + 1) * LANES] = norm_rope_a(
            proj[:, OFF_QA + c * LANES:OFF_QA + (c + 1) * LANES], gq).astype(BF16)
    ka_ref[...] = norm_rope_a(proj[:, OFF_KA:OFF_KA + LANES], gk_ref[...]).astype(BF16)
    va_ref[...] = proj[:, OFF_VA:OFF_VA + LANES].astype(BF16)

    def rms(v, g):
        return v * lax.rsqrt(jnp.mean(v * v, axis=-1, keepdims=True) + RMS_EPS) * g

    cqn = rms(proj[:, OFF_CQ:OFF_CQ + B_Q_RANK], gcq_ref[...]).astype(BF16)
    ckvn = rms(proj[:, OFF_CKV:OFF_CKV + B_KV_RANK], gckv_ref[...]).astype(BF16)
    qb = jnp.dot(cqn, wuq_ref[...], preferred_element_type=F32)
    kn = jnp.dot(ckvn, wuk_ref[...], preferred_element_type=F32)
    vb_ref[...] = jnp.dot(ckvn, wuv_ref[...], preferred_element_type=F32).astype(BF16)

    kpe = proj[:, OFF_KPE:OFF_KPE + LANES]
    kpe = kpe * cb + _rot_b(kpe, lane) * sb
    q_scale = B_QK_DIM ** -0.5
    for h in range(B_HEADS):
        xh = qb[:, h * LANES:(h + 1) * LANES]
        qb_ref[:, h * LANES:(h + 1) * LANES] = ((xh * cb + _rot_b(xh, lane) * sb) * q_scale).astype(BF16)
        kb_ref[:, h * LANES:(h + 1) * LANES] = (kn[:, h * LANES:(h + 1) * LANES] + kpe).astype(BF16)


def _attention_kernel(qa_ref, ka_ref, va_ref, qb_ref, kb_ref, vb_ref,
                      kam_ref, vam_ref, kbm_ref, vbm_ref, ga_ref, gb_ref, o_ref):
    tq = qa_ref.shape[0]
    lane = lax.broadcasted_iota(jnp.int32, (tq, LANES), 1)
    lo = lane < A_HEAD_DIM
    nt = (((1,), (1,)), ((), ()))

    def attend(q, k, v, km, vm):
        s = lax.dot_general(q, k, nt, preferred_element_type=F32)
        sm = lax.dot_general(q, km, nt, preferred_element_type=F32)
        m = jnp.maximum(jnp.max(s, axis=-1, keepdims=True), jnp.max(sm, axis=-1, keepdims=True))
        p = jnp.exp(s - m)
        pm = jnp.exp(sm - m)
        denom = jnp.sum(p, axis=-1, keepdims=True) + jnp.sum(pm, axis=-1, keepdims=True)
        o = (jnp.dot(p.astype(BF16), v, preferred_element_type=F32)
             + jnp.dot(pm.astype(BF16), vm, preferred_element_type=F32))
        return o * (1.0 / denom)

    def group_rms(cols, g):
        ss = sum(jnp.sum(c * c, axis=-1, keepdims=True) for c in cols)
        r = lax.rsqrt(ss * (1.0 / (len(cols) * LANES)) + RMS_EPS)
        return [c * r * g[:, i * LANES:(i + 1) * LANES] for i, c in enumerate(cols)]

    ka, va, kam, vam = ka_ref[...], va_ref[...], kam_ref[...], vam_ref[...]
    cols_a = []
    for c in range(A_WIDTH // LANES):
        g = (2 * c) // A_GROUP
        qc = qa_ref[:, c * LANES:(c + 1) * LANES].astype(F32)
        qsw = pltpu.roll(qc, A_HEAD_DIM, 1)
        keep = lo if g == 0 else jnp.logical_not(lo)
        q_even = jnp.where(keep, qc if g == 0 else qsw, 0.0).astype(BF16)
        q_odd = jnp.where(keep, qsw if g == 0 else qc, 0.0).astype(BF16)
        o_even = attend(q_even, ka, va, kam, vam)
        o_odd = attend(q_odd, ka, va, kam, vam)
        if g == 0:
            cols_a.append(jnp.where(lo, o_even, pltpu.roll(o_odd, A_HEAD_DIM, 1)))
        else:
            cols_a.append(jnp.where(lo, pltpu.roll(o_even, A_HEAD_DIM, 1), o_odd))
    for i, c in enumerate(group_rms(cols_a, ga_ref[...])):
        o_ref[:, i * LANES:(i + 1) * LANES] = c.astype(BF16)

    cols_b = []
    for c in range(B_WIDTH // LANES):
        v2 = vb_ref[:, c * LANES:(c + 1) * LANES]
        vm2 = vbm_ref[:, c * LANES:(c + 1) * LANES]
        halves = []
        for h in (2 * c, 2 * c + 1):
            halves.append(attend(qb_ref[:, h * LANES:(h + 1) * LANES], kb_ref[:, h * LANES:(h + 1) * LANES], v2,
                                 kbm_ref[:, h * LANES:(h + 1) * LANES], vm2))
        cols_b.append(jnp.where(lo, halves[0], halves[1]))
    for i, c in enumerate(group_rms(cols_b, gb_ref[...])):
        o_ref[:, A_WIDTH + i * LANES:A_WIDTH + (i + 1) * LANES] = c.astype(BF16)


def _out_ffn_kernel(o_ref, h1_ref, wout_ref, g2_ref, b2_ref, w1_ref, w3_ref, w2_ref, g3_ref, b3_ref,
                    out_ref, acc_ref):
    mix = jnp.dot(o_ref[...], wout_ref[...], preferred_element_type=F32)
    h2 = _layer_norm(ALPHA * h1_ref[...] + mix, g2_ref[...], b2_ref[...])
    y = _swiglu(h2.astype(BF16), w1_ref, w3_ref, w2_ref, acc_ref)
    out_ref[...] = _layer_norm(ALPHA * h2 + 0.5 * y, g3_ref[...], b3_ref[...])


def _resident(shape):
    return pl.BlockSpec(shape, lambda *_: (0,) * len(shape), pipeline_mode=pl.Buffered(1))


def _ffn_proj_call(x2d, tm, ffn_w, ln, win, gains, wu, tables):
    rows, d = x2d.shape
    row = lambda w: pl.BlockSpec((tm, w), lambda i: (i, 0))
    resident = [*ffn_w, *ln, win, *gains, *wu, *tables]
    out_widths = (d, A_WIDTH, LANES, LANES, B_HEADS * LANES, B_HEADS * LANES, B_WIDTH)
    return pl.pallas_call(
        _ffn_proj_kernel,
        grid=(rows // tm,),
        in_specs=[row(d)] + [_resident(a.shape) for a in resident],
        out_specs=[row(w) for w in out_widths],
        out_shape=[jax.ShapeDtypeStruct((rows, w), F32 if i == 0 else BF16) for i, w in enumerate(out_widths)],
        scratch_shapes=[pltpu.VMEM((tm, d), F32)],
        compiler_params=pltpu.CompilerParams(dimension_semantics=("parallel",), vmem_limit_bytes=VMEM_LIMIT_BYTES),
        name="ffn_proj",
    )(x2d, *resident)


def _attention_call(bsz, seq, tq, qa, ka, va, qb, kb, vb, meta_kv, ga, gb):
    nq = seq // tq
    qspec = lambda w: pl.BlockSpec((tq, w), lambda b, j: (b * nq + j, 0))
    kvspec = lambda w: pl.BlockSpec((seq, w), lambda b, j: (b, 0))
    small = [*meta_kv, ga, gb]
    return pl.pallas_call(
        _attention_kernel,
        grid=(bsz, nq),
        in_specs=[qspec(qa.shape[1]), kvspec(ka.shape[1]), kvspec(va.shape[1]),
                  qspec(qb.shape[1]), kvspec(kb.shape[1]), kvspec(vb.shape[1])]
                 + [pl.BlockSpec(a.shape, lambda b, j: (0, 0)) for a in small],
        out_specs=qspec(A_WIDTH + B_WIDTH),
        out_shape=jax.ShapeDtypeStruct((bsz * seq, A_WIDTH + B_WIDTH), BF16),
        compiler_params=pltpu.CompilerParams(dimension_semantics=("parallel", "arbitrary"),
                                             vmem_limit_bytes=VMEM_LIMIT_BYTES),
        name="attention",
    )(qa, ka, va, qb, kb, vb, *small)


def _out_ffn_call(o, h1, tm, wout, ln2, ffn_w, ln3):
    rows, d = h1.shape
    row = lambda w: pl.BlockSpec((tm, w), lambda i: (i, 0))
    resident = [wout, *ln2, *ffn_w, *ln3]
    return pl.pallas_call(
        _out_ffn_kernel,
        grid=(rows // tm,),
        in_specs=[row(o.shape[1]), row(d)] + [_resident(a.shape) for a in resident],
        out_specs=row(d),
        out_shape=jax.ShapeDtypeStruct((rows, d), F32),
        scratch_shapes=[pltpu.VMEM((tm, d), F32)],
        compiler_params=pltpu.CompilerParams(dimension_semantics=("parallel",), vmem_limit_bytes=VMEM_LIMIT_BYTES),
        name="out_ffn",
    )(o, h1, *resident)


def _ffn_weights(w1, w3, w2):
    d, f = w1.shape
    nc = f // FFN_CHUNK
    up = lambda w: w.astype(BF16).reshape(d, nc, FFN_CHUNK).transpose(1, 0, 2)
    return up(w1), up(w3), w2.astype(BF16).reshape(nc, FFN_CHUNK, d)


def _rope_tables(seq):
    rows = jnp.repeat(jnp.arange(seq // GRID_W, dtype=jnp.int32), GRID_W).astype(F32)[:, None]
    cols = jnp.tile(jnp.arange(GRID_W, dtype=jnp.int32), seq // GRID_W).astype(F32)[:, None]

    def cos_sin(rot_dim):
        axis_dim = rot_dim // 2
        inv_freq = ROPE_THETA ** (-jnp.arange(0, axis_dim, 2, dtype=F32) / axis_dim)
        ang = jnp.concatenate([rows * inv_freq, cols * inv_freq], axis=-1)
        return jnp.cos(ang), jnp.sin(ang)

    cos_a, sin_a = cos_sin(A_HEAD_DIM)
    ca = jnp.tile(cos_a, (1, LANES // cos_a.shape[1]))
    sa = jnp.tile(jnp.concatenate([-sin_a, sin_a], axis=-1), (1, LANES // A_HEAD_DIM))
    cos_b, sin_b = cos_sin(B_ROPE_DIM)
    pad = LANES - B_QK_DIM
    cb = jnp.concatenate([jnp.ones((seq, B_NOPE_DIM), F32), cos_b, cos_b, jnp.ones((seq, pad), F32)], axis=-1)
    sb = jnp.concatenate([jnp.zeros((seq, B_NOPE_DIM), F32), -sin_b, sin_b, jnp.zeros((seq, pad), F32)], axis=-1)
    return ca, sa, cb, sb


def kernel(x, meta_tokens, ffn1_w1, ffn1_w3, ffn1_w2, ln1_g, ln1_b, w_in, q_norm_a, k_norm_a, cq_norm, ckv_norm, w_uq, w_ukv, out_norm_a, out_norm_b, w_out, ln2_g, ln2_b, ffn2_w1, ffn2_w3, ffn2_w2, ln3_g, ln3_b):
    bsz, seq, d = x.shape
    assert seq % ROW_TILE == 0 and seq % Q_TILE == 0 and seq % GRID_W == 0
    assert ffn1_w1.shape[2] % FFN_CHUNK == 0 and d % LANES == 0
    i = 0

    ffn1 = _ffn_weights(ffn1_w1[i], ffn1_w3[i], ffn1_w2[i])
    ffn2 = _ffn_weights(ffn2_w1[i], ffn2_w3[i], ffn2_w2[i])
    row_vec = lambda v: v.reshape(1, -1).astype(F32)
    win = w_in[i]
    n_main = OFF_KPE
    kpe_cols = jnp.pad(win[:, n_main:], ((0, 0), (B_NOPE_DIM, LANES - B_QK_DIM)))
    win_p = jnp.concatenate([win[:, :n_main], kpe_cols], axis=1).astype(BF16)
    wuq_p = jnp.pad(w_uq[i].reshape(B_Q_RANK, B_HEADS, B_QK_DIM),
                    ((0, 0), (0, 0), (0, LANES - B_QK_DIM))).reshape(B_Q_RANK, B_HEADS * LANES).astype(BF16)
    wukv = w_ukv[i].reshape(B_KV_RANK, B_HEADS, B_NOPE_DIM + B_V_DIM)
    wuk_p = jnp.pad(wukv[:, :, :B_NOPE_DIM],
                    ((0, 0), (0, 0), (0, LANES - B_NOPE_DIM))).reshape(B_KV_RANK, B_HEADS * LANES).astype(BF16)
    wuv = wukv[:, :, B_NOPE_DIM:].reshape(B_KV_RANK, B_WIDTH).astype(BF16)
    gq = jnp.tile(q_norm_a[i] * (A_HEAD_DIM ** -0.5), LANES // A_HEAD_DIM).reshape(1, LANES).astype(F32)
    gk = jnp.tile(k_norm_a[i], LANES // A_HEAD_DIM).reshape(1, LANES).astype(F32)
    gains = (gq, gk, row_vec(cq_norm[i]), row_vec(ckv_norm[i]))
    ln1 = (row_vec(ln1_g[i]), row_vec(ln1_b[i]))
    wu = (wuq_p, wuk_p, wuv)

    tables = _rope_tables(seq)
    ones, zeros = jnp.ones((N_META, LANES), F32), jnp.zeros((N_META, LANES), F32)
    meta_tables = (ones, zeros, ones, zeros)

    x2d = x.reshape(bsz * seq, d)
    h1, qa, ka, va, qb, kb, vb = _ffn_proj_call(x2d, ROW_TILE, ffn1, ln1, win_p, gains, wu, tables)
    _, _, kam, vam, _, kbm, vbm = _ffn_proj_call(meta_tokens.astype(F32), N_META, ffn1, ln1, win_p, gains, wu,
                                                 meta_tables)

    o = _attention_call(bsz, seq, Q_TILE, qa, ka, va, qb, kb, vb, (kam, vam, kbm, vbm),
                        row_vec(out_norm_a[i]), row_vec(out_norm_b[i]))

    out = _out_ffn_call(o, h1, ROW_TILE, w_out[i].astype(BF16), (row_vec(ln2_g[i]), row_vec(ln2_b[i])),
                        ffn2, (row_vec(ln3_g[i]), row_vec(ln3_b[i])))
    return out.reshape(bsz, seq, d)
```

```python
import math

import jax
import jax.numpy as jnp
from jax import lax
from jax.experimental import pallas as pl
from jax.experimental.pallas import tpu as pltpu

F32 = jnp.float32
BF16 = jnp.bfloat16

LANES = 128
BF16_SUBLANES = 16
VMEM_LIMIT_BYTES = 56 * 1024 * 1024

N_META = 16
GRID_W = 64
ROPE_THETA = 10000.0
RMS_EPS = 1e-6
LN_EPS = 1e-5
A_HEADS, A_KV_HEADS, A_HEAD_DIM = 8, 2, 64
A_GROUP = A_HEADS // A_KV_HEADS
B_HEADS, B_Q_RANK, B_KV_RANK = 8, 256, 128
B_NOPE_DIM, B_ROPE_DIM, B_V_DIM = 64, 32, 64
B_QK_DIM = B_NOPE_DIM + B_ROPE_DIM
A_WIDTH = A_HEADS * A_HEAD_DIM
B_WIDTH = B_HEADS * B_V_DIM
DEPTH = 1
ALPHA = (2.0 * DEPTH) ** 0.25
LOG2_E = math.log2(math.e)

OFF_QA, OFF_KA, OFF_VA = 0, A_WIDTH, A_WIDTH + LANES
OFF_CQ = OFF_VA + LANES
OFF_CKV = OFF_CQ + B_Q_RANK
OFF_KPE = OFF_CKV + B_KV_RANK
IN_PAD_WIDTH = OFF_KPE + LANES

FFN_CHUNK = 256
ROW_TILE = 512
Q_TILE = 256
KEY_CHUNK = 512


def _layer_norm(z, g, b):
    mu = jnp.mean(z, axis=-1, keepdims=True)
    zc = z - mu
    var = jnp.mean(zc * zc, axis=-1, keepdims=True)
    return zc * lax.rsqrt(var + LN_EPS) * g + b


def _swiglu(xb, w1_ref, w3_ref, w2_ref, acc_ref):
    acc_ref[...] = jnp.zeros_like(acc_ref)

    def body(c, carry):
        u = jnp.dot(xb, w1_ref[c], preferred_element_type=F32)
        g = jnp.dot(xb, w3_ref[c], preferred_element_type=F32)
        a = (u * jax.nn.sigmoid(u) * g).astype(BF16)
        acc_ref[...] += jnp.dot(a, w2_ref[c], preferred_element_type=F32)
        return carry

    lax.fori_loop(0, w1_ref.shape[0], body, 0)
    return acc_ref[...]


def _rot_a(n, lane):
    return jnp.where((lane % A_HEAD_DIM) < A_HEAD_DIM // 2,
                     pltpu.roll(n, LANES - A_HEAD_DIM // 2, 1), pltpu.roll(n, A_HEAD_DIM // 2, 1))


def _rot_b(x, lane):
    half = B_ROPE_DIM // 2
    return jnp.where(lane < B_NOPE_DIM + half, pltpu.roll(x, LANES - half, 1), pltpu.roll(x, half, 1))


def _ffn_proj_kernel(x_ref, w1_ref, w3_ref, w2_ref, lng_ref, lnb_ref, win_ref,
                     gq_ref, gk_ref, gcq_ref, gckv_ref, wuq_ref, wuk_ref, wuv_ref,
                     ca_ref, sa_ref, cb_ref, sb_ref,
                     h1_ref, qa_ref, ka_ref, vat_ref, qb_ref, kb_ref, vbt_ref, acc_ref):
    tm = x_ref.shape[0]
    x = x_ref[...]
    y = _swiglu(x.astype(BF16), w1_ref, w3_ref, w2_ref, acc_ref)
    h1 = _layer_norm(ALPHA * x + 0.5 * y, lng_ref[...], lnb_ref[...])
    h1_ref[...] = h1

    proj = jnp.dot(h1.astype(BF16), win_ref[...], preferred_element_type=F32)

    n_pos_blocks = ca_ref.shape[0] // tm
    row0 = pl.multiple_of((pl.program_id(0) % n_pos_blocks) * tm, tm)
    ca = ca_ref[pl.ds(row0, tm), :]
    sa = sa_ref[pl.ds(row0, tm), :]
    cb = cb_ref[pl.ds(row0, tm), :]
    sb = sb_ref[pl.ds(row0, tm), :]

    lane = lax.broadcasted_iota(jnp.int32, (tm, LANES), 1)
    lo = lane < A_HEAD_DIM

    def norm_rope_a(xc, g):
        x2 = xc * xc
        s_lo = jnp.sum(jnp.where(lo, x2, 0.0), axis=-1, keepdims=True)
        s_hi = jnp.sum(jnp.where(lo, 0.0, x2), axis=-1, keepdims=True)
        ms = jnp.where(lo, s_lo, s_hi) * (1.0 / A_HEAD_DIM)
        n = xc * lax.rsqrt(ms + RMS_EPS) * g
        return n * ca + _rot_a(n, lane) * sa

    gq = gq_ref[...]
    for c in range(A_WIDTH // LANES):
        qa_ref[:, c * LANES:(c + 1) * LANES] = norm_rope_a(
            proj[:, OFF_QA + c * LANES:OFF_QA + (c + 1) * LANES], gq).astype(BF16)
    ka_ref[...] = norm_rope_a(proj[:, OFF_KA:OFF_KA + LANES], gk_ref[...]).astype(BF16)
    vat_ref[...] = proj[:, OFF_VA:OFF_VA + LANES].T.astype(BF16)

    def rms(v, g):
        return v * lax.rsqrt(jnp.mean(v * v, axis=-1, keepdims=True) + RMS_EPS) * g

    cqn = rms(proj[:, OFF_CQ:OFF_CQ + B_Q_RANK], gcq_ref[...]).astype(BF16)
    ckvn = rms(proj[:, OFF_CKV:OFF_CKV + B_KV_RANK], gckv_ref[...]).astype(BF16)
    qb = jnp.dot(cqn, wuq_ref[...], preferred_element_type=F32)
    kn = jnp.dot(ckvn, wuk_ref[...], preferred_element_type=F32)
    vbt_ref[...] = jnp.dot(ckvn, wuv_ref[...], preferred_element_type=F32).T.astype(BF16)

    kpe = proj[:, OFF_KPE:OFF_KPE + LANES]
    kpe = kpe * cb + _rot_b(kpe, lane) * sb
    q_scale = B_QK_DIM ** -0.5 * LOG2_E
    for h in range(B_HEADS):
        xh = qb[:, h * LANES:(h + 1) * LANES]
        qb_ref[:, h * LANES:(h + 1) * LANES] = ((xh * cb + _rot_b(xh, lane) * sb) * q_scale).astype(BF16)
        kb_ref[:, h * LANES:(h + 1) * LANES] = (kn[:, h * LANES:(h + 1) * LANES] + kpe).astype(BF16)


def _attention_kernel(qa_ref, ka_ref, vat_ref, qb_ref, kb_ref, vbt_ref,
                      kam_ref, vamt_ref, kbm_ref, vbmt_ref, ga_ref, gb_ref, o_ref, s0_ref, s1_ref):
    tq = qa_ref.shape[0]
    n_keys = ka_ref.shape[0]
    n_chunks = n_keys // KEY_CHUNK
    hd = A_HEAD_DIM
    s_refs = (s0_ref, s1_ref)
    lane = lax.broadcasted_iota(jnp.int32, (tq, LANES), 1)
    lo = lane < A_HEAD_DIM
    ones = jnp.ones((BF16_SUBLANES, KEY_CHUNK), BF16)
    zero = jnp.minimum(pl.program_id(1), 0)
    meta_rows = pl.ds(pl.multiple_of(zero + n_keys, N_META), N_META)

    def chunk_rows(j):
        return pl.ds(pl.multiple_of(zero + j * KEY_CHUNK, KEY_CHUNK), KEY_CHUNK)

    def q_a(h):
        c, g = h // 2, h // A_GROUP
        qc = qa_ref[:, c * LANES:(c + 1) * LANES].astype(F32)
        on_kv_lanes = qc if (h % 2) == g else pltpu.roll(qc, A_HEAD_DIM, 1)
        return jnp.where(lo if g == 0 else jnp.logical_not(lo), on_kv_lanes, 0.0)

    heads = []
    for h in range(A_HEADS):
        g = h // A_GROUP
        heads.append((lambda h=h: q_a(h), lambda: ka_ref[...],
                      lambda g=g: vat_ref[g * hd:(g + 1) * hd, :], lambda: kam_ref[...],
                      lambda g=g: vamt_ref[g * hd:(g + 1) * hd, :]))
    for h in range(B_HEADS):
        heads.append((lambda h=h: qb_ref[:, h * LANES:(h + 1) * LANES].astype(F32),
                      lambda h=h: kb_ref[:, h * LANES:(h + 1) * LANES],
                      lambda h=h: vbt_ref[h * hd:(h + 1) * hd, :],
                      lambda h=h: kbm_ref[:, h * LANES:(h + 1) * LANES],
                      lambda h=h: vbmt_ref[h * hd:(h + 1) * hd, :]))

    def scores_begin(i):
        qt = heads[i][0]().T.astype(BF16)
        smt = jnp.dot(heads[i][3](), qt, preferred_element_type=F32)
        s_refs[i % 2][meta_rows, :] = smt
        return qt, jnp.max(smt, axis=0, keepdims=True)

    def scores_chunk(i, j, qt, m):
        k = heads[i][1]()
        st = jnp.dot(k[j * KEY_CHUNK:(j + 1) * KEY_CHUNK, :], qt, preferred_element_type=F32)
        s_refs[i % 2][chunk_rows(j), :] = st
        return jnp.maximum(m, jnp.max(st, axis=0, keepdims=True))

    def values_begin(i, m):
        pm = jnp.exp2(s_refs[i % 2][meta_rows, :] - m)
        lhs = jnp.concatenate([heads[i][4](), ones[:, :N_META]], axis=0)
        return jnp.dot(lhs, pm.astype(BF16), preferred_element_type=F32)

    def values_chunk(i, j, m, ot):
        p = jnp.exp2(s_refs[i % 2][chunk_rows(j), :] - m)
        lhs = jnp.concatenate([heads[i][2]()[:, j * KEY_CHUNK:(j + 1) * KEY_CHUNK], ones], axis=0)
        return ot + jnp.dot(lhs, p.astype(BF16), preferred_element_type=F32)

    outs = []
    qt, m_next = scores_begin(0)
    for j in range(n_chunks):
        m_next = scores_chunk(0, j, qt, m_next)
    for i in range(len(heads)):
        m = m_next
        more = i + 1 < len(heads)
        if more:
            qt, m_next = scores_begin(i + 1)
        ot = values_begin(i, m)
        for j in range(n_chunks):
            if more:
                m_next = scores_chunk(i + 1, j, qt, m_next)
            ot = values_chunk(i, j, m, ot)
        outs.append(ot[:hd] * (1.0 / ot[hd:hd + 1]))

    def group_rms(head_rows, g):
        xt = jnp.concatenate(head_rows, axis=0)
        r = lax.rsqrt(jnp.mean(xt * xt, axis=0, keepdims=True) + RMS_EPS)
        return (xt * r * g).T.astype(BF16)

    o_ref[:, :A_WIDTH] = group_rms(outs[:A_HEADS], ga_ref[...])
    o_ref[:, A_WIDTH:] = group_rms(outs[A_HEADS:], gb_ref[...])


def _out_ffn_kernel(o_ref, h1_ref, wout_ref, g2_ref, b2_ref, w1_ref, w3_ref, w2_ref, g3_ref, b3_ref,
                    out_ref, acc_ref):
    mix = jnp.dot(o_ref[...], wout_ref[...], preferred_element_type=F32)
    h2 = _layer_norm(ALPHA * h1_ref[...] + mix, g2_ref[...], b2_ref[...])
    y = _swiglu(h2.astype(BF16), w1_ref, w3_ref, w2_ref, acc_ref)
    out_ref[...] = _layer_norm(ALPHA * h2 + 0.5 * y, g3_ref[...], b3_ref[...])


def _resident(shape):
    return pl.BlockSpec(shape, lambda *_: (0,) * len(shape), pipeline_mode=pl.Buffered(1))


def _ffn_proj_call(x2d, seq, tm, ffn_w, ln, win, gains, wu, tables):
    rows, d = x2d.shape
    n_pos_blocks = seq // tm
    row = lambda w: pl.BlockSpec((tm, w), lambda i: (i, 0))
    col = lambda w: pl.BlockSpec((w, tm), lambda i: (i // n_pos_blocks, i % n_pos_blocks))
    resident = [*ffn_w, *ln, win, *gains, *wu, *tables]
    tok = lambda w, dt=BF16: jax.ShapeDtypeStruct((rows, w), dt)
    feat = lambda w: jax.ShapeDtypeStruct((rows // seq * w, seq), BF16)
    return pl.pallas_call(
        _ffn_proj_kernel,
        grid=(rows // tm,),
        in_specs=[row(d)] + [_resident(a.shape) for a in resident],
        out_specs=[row(d), row(A_WIDTH), row(LANES), col(LANES), row(B_HEADS * LANES), row(B_HEADS * LANES),
                   col(B_WIDTH)],
        out_shape=[tok(d, F32), tok(A_WIDTH), tok(LANES), feat(LANES), tok(B_HEADS * LANES), tok(B_HEADS * LANES),
                   feat(B_WIDTH)],
        scratch_shapes=[pltpu.VMEM((tm, d), F32)],
        compiler_params=pltpu.CompilerParams(dimension_semantics=("parallel",), vmem_limit_bytes=VMEM_LIMIT_BYTES),
        name="ffn_proj",
    )(x2d, *resident)


def _attention_call(bsz, seq, tq, qa, ka, vat, qb, kb, vbt, meta_kv, ga, gb):
    nq = seq // tq
    qspec = lambda w: pl.BlockSpec((tq, w), lambda b, j: (b * nq + j, 0))
    kspec = lambda w: pl.BlockSpec((seq, w), lambda b, j: (b, 0))
    vspec = lambda w: pl.BlockSpec((w, seq), lambda b, j: (b, 0))
    small = [*meta_kv, ga, gb]
    return pl.pallas_call(
        _attention_kernel,
        grid=(bsz, nq),
        in_specs=[qspec(qa.shape[1]), kspec(ka.shape[1]), vspec(LANES),
                  qspec(qb.shape[1]), kspec(kb.shape[1]), vspec(B_WIDTH)]
                 + [pl.BlockSpec(a.shape, lambda b, j: (0, 0)) for a in small],
        out_specs=qspec(A_WIDTH + B_WIDTH),
        out_shape=jax.ShapeDtypeStruct((bsz * seq, A_WIDTH + B_WIDTH), BF16),
        scratch_shapes=[pltpu.VMEM((seq + N_META, tq), F32) for _ in range(2)],
        compiler_params=pltpu.CompilerParams(dimension_semantics=("parallel", "arbitrary"),
                                             vmem_limit_bytes=VMEM_LIMIT_BYTES),
        name="attention",
    )(qa, ka, vat, qb, kb, vbt, *small)


def _out_ffn_call(o, h1, tm, wout, ln2, ffn_w, ln3):
    rows, d = h1.shape
    row = lambda w: pl.BlockSpec((tm, w), lambda i: (i, 0))
    resident = [wout, *ln2, *ffn_w, *ln3]
    return pl.pallas_call(
        _out_ffn_kernel,
        grid=(rows // tm,),
        in_specs=[row(o.shape[1]), row(d)] + [_resident(a.shape) for a in resident],
        out_specs=row(d),
        out_shape=jax.ShapeDtypeStruct((rows, d), F32),
        scratch_shapes=[pltpu.VMEM((tm, d), F32)],
        compiler_params=pltpu.CompilerParams(dimension_semantics=("parallel",), vmem_limit_bytes=VMEM_LIMIT_BYTES),
        name="out_ffn",
    )(o, h1, *resident)


def _ffn_weights(w1, w3, w2):
    d, f = w1.shape
    nc = f // FFN_CHUNK
    up = lambda w: w.astype(BF16).reshape(d, nc, FFN_CHUNK).transpose(1, 0, 2)
    return up(w1), up(w3), w2.astype(BF16).reshape(nc, FFN_CHUNK, d)


def _rope_tables(seq):
    rows = jnp.repeat(jnp.arange(seq // GRID_W, dtype=jnp.int32), GRID_W).astype(F32)[:, None]
    cols = jnp.tile(jnp.arange(GRID_W, dtype=jnp.int32), seq // GRID_W).astype(F32)[:, None]

    def cos_sin(rot_dim):
        axis_dim = rot_dim // 2
        inv_freq = ROPE_THETA ** (-jnp.arange(0, axis_dim, 2, dtype=F32) / axis_dim)
        ang = jnp.concatenate([rows * inv_freq, cols * inv_freq], axis=-1)
        return jnp.cos(ang), jnp.sin(ang)

    cos_a, sin_a = cos_sin(A_HEAD_DIM)
    ca = jnp.tile(cos_a, (1, LANES // cos_a.shape[1]))
    sa = jnp.tile(jnp.concatenate([-sin_a, sin_a], axis=-1), (1, LANES // A_HEAD_DIM))
    cos_b, sin_b = cos_sin(B_ROPE_DIM)
    pad = LANES - B_QK_DIM
    cb = jnp.concatenate([jnp.ones((seq, B_NOPE_DIM), F32), cos_b, cos_b, jnp.ones((seq, pad), F32)], axis=-1)
    sb = jnp.concatenate([jnp.zeros((seq, B_NOPE_DIM), F32), -sin_b, sin_b, jnp.zeros((seq, pad), F32)], axis=-1)
    return ca, sa, cb, sb


def kernel(x, meta_tokens, ffn1_w1, ffn1_w3, ffn1_w2, ln1_g, ln1_b, w_in, q_norm_a, k_norm_a, cq_norm, ckv_norm, w_uq, w_ukv, out_norm_a, out_norm_b, w_out, ln2_g, ln2_b, ffn2_w1, ffn2_w3, ffn2_w2, ln3_g, ln3_b):
    bsz, seq, d = x.shape
    assert seq % ROW_TILE == 0 and seq % Q_TILE == 0 and seq % GRID_W == 0 and seq % KEY_CHUNK == 0
    assert ffn1_w1.shape[2] % FFN_CHUNK == 0 and d % LANES == 0
    i = 0

    ffn1 = _ffn_weights(ffn1_w1[i], ffn1_w3[i], ffn1_w2[i])
    ffn2 = _ffn_weights(ffn2_w1[i], ffn2_w3[i], ffn2_w2[i])
    row_vec = lambda v: v.reshape(1, -1).astype(F32)
    col_vec = lambda v: v.reshape(-1, 1).astype(F32)
    win = w_in[i]
    n_main = OFF_KPE
    kpe_cols = jnp.pad(win[:, n_main:], ((0, 0), (B_NOPE_DIM, LANES - B_QK_DIM)))
    win_p = jnp.concatenate([win[:, :n_main], kpe_cols], axis=1).astype(BF16)
    wuq_p = jnp.pad(w_uq[i].reshape(B_Q_RANK, B_HEADS, B_QK_DIM),
                    ((0, 0), (0, 0), (0, LANES - B_QK_DIM))).reshape(B_Q_RANK, B_HEADS * LANES).astype(BF16)
    wukv = w_ukv[i].reshape(B_KV_RANK, B_HEADS, B_NOPE_DIM + B_V_DIM)
    wuk_p = jnp.pad(wukv[:, :, :B_NOPE_DIM],
                    ((0, 0), (0, 0), (0, LANES - B_NOPE_DIM))).reshape(B_KV_RANK, B_HEADS * LANES).astype(BF16)
    wuv = wukv[:, :, B_NOPE_DIM:].reshape(B_KV_RANK, B_WIDTH).astype(BF16)
    gq = jnp.tile(q_norm_a[i] * (A_HEAD_DIM ** -0.5 * LOG2_E), LANES // A_HEAD_DIM).reshape(1, LANES).astype(F32)
    gk = jnp.tile(k_norm_a[i], LANES // A_HEAD_DIM).reshape(1, LANES).astype(F32)
    gains = (gq, gk, row_vec(cq_norm[i]), row_vec(ckv_norm[i]))
    ln1 = (row_vec(ln1_g[i]), row_vec(ln1_b[i]))
    wu = (wuq_p, wuk_p, wuv)

    tables = _rope_tables(seq)
    ones, zeros = jnp.ones((N_META, LANES), F32), jnp.zeros((N_META, LANES), F32)
    meta_tables = (ones, zeros, ones, zeros)

    x2d = x.reshape(bsz * seq, d)
    h1, qa, ka, vat, qb, kb, vbt = _ffn_proj_call(x2d, seq, ROW_TILE, ffn1, ln1, win_p, gains, wu, tables)
    _, _, kam, vamt, _, kbm, vbmt = _ffn_proj_call(meta_tokens.astype(F32), N_META, N_META, ffn1, ln1, win_p, gains,
                                                   wu, meta_tables)

    o = _attention_call(bsz, seq, Q_TILE, qa, ka, vat, qb, kb, vbt, (kam, vamt, kbm, vbmt),
                        col_vec(out_norm_a[i]), col_vec(out_norm_b[i]))

    out = _out_ffn_call(o, h1, ROW_TILE, w_out[i].astype(BF16), (row_vec(ln2_g[i]), row_vec(ln2_b[i])),
                        ffn2, (row_vec(ln3_g[i]), row_vec(ln3_b[i])))
    return out.reshape(bsz, seq, d)
```

```python
import math

import jax
import jax.numpy as jnp
from jax import lax
from jax.experimental import pallas as pl
from jax.experimental.pallas import tpu as pltpu

F32 = jnp.float32
BF16 = jnp.bfloat16

LANES = 128
BF16_SUBLANES = 16
VMEM_LIMIT_BYTES = 56 * 1024 * 1024

N_META = 16
GRID_W = 64
ROPE_THETA = 10000.0
RMS_EPS = 1e-6
LN_EPS = 1e-5
A_HEADS, A_KV_HEADS, A_HEAD_DIM = 8, 2, 64
A_GROUP = A_HEADS // A_KV_HEADS
B_HEADS, B_Q_RANK, B_KV_RANK = 8, 256, 128
B_NOPE_DIM, B_ROPE_DIM, B_V_DIM = 64, 32, 64
B_QK_DIM = B_NOPE_DIM + B_ROPE_DIM
A_WIDTH = A_HEADS * A_HEAD_DIM
B_WIDTH = B_HEADS * B_V_DIM
DEPTH = 1
ALPHA = (2.0 * DEPTH) ** 0.25
LOG2_E = math.log2(math.e)

OFF_QA, OFF_KA, OFF_VA = 0, A_WIDTH, A_WIDTH + LANES
OFF_CQ = OFF_VA + LANES
OFF_CKV = OFF_CQ + B_Q_RANK
OFF_KPE = OFF_CKV + B_KV_RANK
IN_PAD_WIDTH = OFF_KPE + LANES

FFN_CHUNK = 256
ROW_TILE = 512
Q_TILE = 512
KEY_CHUNK = 512


def _layer_norm(z, g, b):
    mu = jnp.mean(z, axis=-1, keepdims=True)
    zc = z - mu
    var = jnp.mean(zc * zc, axis=-1, keepdims=True)
    return zc * lax.rsqrt(var + LN_EPS) * g + b


def _swiglu(xb, w1_ref, w3_ref, w2_ref):
    acc = None
    for c in range(w1_ref.shape[1] // FFN_CHUNK):
        cols = slice(c * FFN_CHUNK, (c + 1) * FFN_CHUNK)
        u = jnp.dot(xb, w1_ref[:, cols], preferred_element_type=F32)
        g = jnp.dot(xb, w3_ref[:, cols], preferred_element_type=F32)
        a = (u * jax.nn.sigmoid(u) * g).astype(BF16)
        y = jnp.dot(a, w2_ref[cols, :], preferred_element_type=F32)
        acc = y if acc is None else acc + y
    return acc


def _rot_b(x, lane):
    half = B_ROPE_DIM // 2
    return jnp.where(lane < B_NOPE_DIM + half, pltpu.roll(x, LANES - half, 1), pltpu.roll(x, half, 1))


def _ffn_proj_kernel(x_ref, w1_ref, w3_ref, w2_ref, lng_ref, lnb_ref, win_ref,
                     gq_ref, gk_ref, gcq_ref, gckv_ref, wuq_ref, wuqr_ref, wuk_ref, wuv_ref,
                     ca_ref, sa_ref, cb_ref, sb_ref,
                     h1_ref, qa_ref, ka_ref, vat_ref, qb_ref, kb_ref, vbt_ref):
    tm = x_ref.shape[0]
    x = x_ref[...]
    y = _swiglu(x.astype(BF16), w1_ref, w3_ref, w2_ref)
    h1 = _layer_norm(ALPHA * x + 0.5 * y, lng_ref[...], lnb_ref[...])
    h1_ref[...] = h1

    proj = jnp.dot(h1.astype(BF16), win_ref[...], preferred_element_type=F32)

    n_pos_blocks = ca_ref.shape[0] // tm
    row0 = pl.multiple_of((pl.program_id(0) % n_pos_blocks) * tm, tm)
    ca = ca_ref[pl.ds(row0, tm), :]
    sa = sa_ref[pl.ds(row0, tm), :]
    cb = cb_ref[pl.ds(row0, tm), :]
    sb = sb_ref[pl.ds(row0, tm), :]

    lane = lax.broadcasted_iota(jnp.int32, (tm, LANES), 1)
    first = (lane % A_HEAD_DIM) < A_HEAD_DIM // 2

    def norm_rope_a(xc, g):
        x2 = xc * xc
        s0 = jnp.sum(jnp.where(first, x2, 0.0), axis=-1, keepdims=True)
        s1 = jnp.sum(jnp.where(first, 0.0, x2), axis=-1, keepdims=True)
        ms = jnp.where(first, s0, s1) * (1.0 / A_HEAD_DIM)
        n = xc * lax.rsqrt(ms + RMS_EPS) * g
        return n * ca + pltpu.roll(n, LANES // 2, 1) * sa

    gq = gq_ref[...]
    for c in range(A_WIDTH // LANES):
        qa_ref[:, c * LANES:(c + 1) * LANES] = norm_rope_a(
            proj[:, OFF_QA + c * LANES:OFF_QA + (c + 1) * LANES], gq).astype(BF16)
    ka_ref[...] = norm_rope_a(proj[:, OFF_KA:OFF_KA + LANES], gk_ref[...]).astype(BF16)
    vat_ref[...] = proj[:, OFF_VA:OFF_VA + LANES].T.astype(BF16)

    def rms(v, g):
        return v * lax.rsqrt(jnp.mean(v * v, axis=-1, keepdims=True) + RMS_EPS) * g

    cqn = rms(proj[:, OFF_CQ:OFF_CQ + B_Q_RANK], gcq_ref[...]).astype(BF16)
    ckvn = rms(proj[:, OFF_CKV:OFF_CKV + B_KV_RANK], gckv_ref[...]).astype(BF16)
    qb = jnp.dot(cqn, wuq_ref[...], preferred_element_type=F32)
    qb_rot = jnp.dot(cqn, wuqr_ref[...], preferred_element_type=F32)
    kn = jnp.dot(ckvn, wuk_ref[...], preferred_element_type=F32)
    vbt_ref[...] = jnp.dot(ckvn, wuv_ref[...], preferred_element_type=F32).T.astype(BF16)

    kpe = proj[:, OFF_KPE:OFF_KPE + LANES]
    kpe = kpe * cb + _rot_b(kpe, lane) * sb
    q_scale = B_QK_DIM ** -0.5 * LOG2_E
    for h in range(B_HEADS):
        xh = qb[:, h * LANES:(h + 1) * LANES]
        qb_ref[:, h * LANES:(h + 1) * LANES] = (
            (xh * cb + qb_rot[:, h * LANES:(h + 1) * LANES] * sb) * q_scale).astype(BF16)
        kb_ref[:, h * LANES:(h + 1) * LANES] = (kn[:, h * LANES:(h + 1) * LANES] + kpe).astype(BF16)


def _attention_kernel(qa_ref, ka_ref, vat_ref, qb_ref, kb_ref, vbt_ref,
                      kam_ref, vamt_ref, kbm_ref, vbmt_ref, ga_ref, gb_ref, o_ref, s0_ref, s1_ref):
    tq = qa_ref.shape[0]
    n_keys = ka_ref.shape[0]
    n_chunks = n_keys // KEY_CHUNK
    hd = A_HEAD_DIM
    s_refs = (s0_ref, s1_ref)
    lane = lax.broadcasted_iota(jnp.int32, (tq, LANES), 1)
    first = (lane % A_HEAD_DIM) < A_HEAD_DIM // 2
    ones = jnp.ones((BF16_SUBLANES, KEY_CHUNK), BF16)
    zero = jnp.minimum(pl.program_id(1), 0)
    meta_rows = pl.ds(pl.multiple_of(zero + n_keys, N_META), N_META)

    def chunk_rows(j):
        return pl.ds(pl.multiple_of(zero + j * KEY_CHUNK, KEY_CHUNK), KEY_CHUNK)

    def q_a(h):
        c, g = h // 2, h // A_GROUP
        half = A_HEAD_DIM // 2
        qc = qa_ref[:, c * LANES:(c + 1) * LANES].astype(F32)
        if (h % 2) == g:
            on_kv_lanes = qc
        else:
            on_kv_lanes = pltpu.roll(qc, half if g == 1 else LANES - half, 1)
        return jnp.where(first if g == 0 else jnp.logical_not(first), on_kv_lanes, 0.0)

    heads = []
    for h in range(A_HEADS):
        g = h // A_GROUP
        heads.append((lambda h=h: q_a(h), lambda: ka_ref[...],
                      lambda g=g: vat_ref[g * hd:(g + 1) * hd, :], lambda: kam_ref[...],
                      lambda g=g: vamt_ref[g * hd:(g + 1) * hd, :]))
    for h in range(B_HEADS):
        heads.append((lambda h=h: qb_ref[:, h * LANES:(h + 1) * LANES].astype(F32),
                      lambda h=h: kb_ref[:, h * LANES:(h + 1) * LANES],
                      lambda h=h: vbt_ref[h * hd:(h + 1) * hd, :],
                      lambda h=h: kbm_ref[:, h * LANES:(h + 1) * LANES],
                      lambda h=h: vbmt_ref[h * hd:(h + 1) * hd, :]))

    def scores_begin(i):
        qt = heads[i][0]().T.astype(BF16)
        smt = jnp.dot(heads[i][3](), qt, preferred_element_type=F32)
        s_refs[i % 2][meta_rows, :] = smt
        return qt, jnp.max(smt, axis=0, keepdims=True)

    def scores_chunk(i, j, qt, m):
        k = heads[i][1]()
        st = jnp.dot(k[j * KEY_CHUNK:(j + 1) * KEY_CHUNK, :], qt, preferred_element_type=F32)
        s_refs[i % 2][chunk_rows(j), :] = st
        return jnp.maximum(m, jnp.max(st, axis=0, keepdims=True))

    def values_begin(i, m):
        pm = jnp.exp2(s_refs[i % 2][meta_rows, :] - m)
        lhs = jnp.concatenate([heads[i][4](), ones[:, :N_META]], axis=0)
        return jnp.dot(lhs, pm.astype(BF16), preferred_element_type=F32)

    def values_chunk(i, j, m, ot):
        p = jnp.exp2(s_refs[i % 2][chunk_rows(j), :] - m)
        lhs = jnp.concatenate([heads[i][2]()[:, j * KEY_CHUNK:(j + 1) * KEY_CHUNK], ones], axis=0)
        return ot + jnp.dot(lhs, p.astype(BF16), preferred_element_type=F32)

    outs = []
    qt, m_next = scores_begin(0)
    for j in range(n_chunks):
        m_next = scores_chunk(0, j, qt, m_next)
    for i in range(len(heads)):
        m = m_next
        more = i + 1 < len(heads)
        if more:
            qt, m_next = scores_begin(i + 1)
        ot = values_begin(i, m)
        for j in range(n_chunks):
            if more:
                m_next = scores_chunk(i + 1, j, qt, m_next)
            ot = values_chunk(i, j, m, ot)
        outs.append(ot[:hd] * (1.0 / ot[hd:hd + 1]))

    def group_rms(head_rows, g):
        xt = jnp.concatenate(head_rows, axis=0)
        r = lax.rsqrt(jnp.mean(xt * xt, axis=0, keepdims=True) + RMS_EPS)
        return (xt * r * g).T.astype(BF16)

    o_ref[:, :A_WIDTH] = group_rms(outs[:A_HEADS], ga_ref[...])
    o_ref[:, A_WIDTH:] = group_rms(outs[A_HEADS:], gb_ref[...])


def _out_ffn_kernel(o_ref, h1_ref, wout_ref, g2_ref, b2_ref, w1_ref, w3_ref, w2_ref, g3_ref, b3_ref,
                    out_ref):
    mix = jnp.dot(o_ref[...], wout_ref[...], preferred_element_type=F32)
    h2 = _layer_norm(ALPHA * h1_ref[...] + mix, g2_ref[...], b2_ref[...])
    y = _swiglu(h2.astype(BF16), w1_ref, w3_ref, w2_ref)
    out_ref[...] = _layer_norm(ALPHA * h2 + 0.5 * y, g3_ref[...], b3_ref[...])


def _resident(shape):
    return pl.BlockSpec(shape, lambda *_: (0,) * len(shape), pipeline_mode=pl.Buffered(1))


def _ffn_proj_call(x2d, seq, tm, ffn_w, ln, win, gains, wu, tables):
    rows, d = x2d.shape
    n_pos_blocks = seq // tm
    row = lambda w: pl.BlockSpec((tm, w), lambda i: (i, 0))
    col = lambda w: pl.BlockSpec((w, tm), lambda i: (i // n_pos_blocks, i % n_pos_blocks))
    resident = [*ffn_w, *ln, win, *gains, *wu, *tables]
    tok = lambda w, dt=BF16: jax.ShapeDtypeStruct((rows, w), dt)
    feat = lambda w: jax.ShapeDtypeStruct((rows // seq * w, seq), BF16)
    return pl.pallas_call(
        _ffn_proj_kernel,
        grid=(rows // tm,),
        in_specs=[row(d)] + [_resident(a.shape) for a in resident],
        out_specs=[row(d), row(A_WIDTH), row(LANES), col(LANES), row(B_HEADS * LANES), row(B_HEADS * LANES),
                   col(B_WIDTH)],
        out_shape=[tok(d, F32), tok(A_WIDTH), tok(LANES), feat(LANES), tok(B_HEADS * LANES), tok(B_HEADS * LANES),
                   feat(B_WIDTH)],
        compiler_params=pltpu.CompilerParams(dimension_semantics=("parallel",), vmem_limit_bytes=VMEM_LIMIT_BYTES),
        name="ffn_proj",
    )(x2d, *resident)


def _attention_call(bsz, seq, tq, qa, ka, vat, qb, kb, vbt, meta_kv, ga, gb):
    nq = seq // tq
    qspec = lambda w: pl.BlockSpec((tq, w), lambda b, j: (b * nq + j, 0))
    kspec = lambda w: pl.BlockSpec((seq, w), lambda b, j: (b, 0))
    vspec = lambda w: pl.BlockSpec((w, seq), lambda b, j: (b, 0))
    small = [*meta_kv, ga, gb]
    return pl.pallas_call(
        _attention_kernel,
        grid=(bsz, nq),
        in_specs=[qspec(qa.shape[1]), kspec(ka.shape[1]), vspec(LANES),
                  qspec(qb.shape[1]), kspec(kb.shape[1]), vspec(B_WIDTH)]
                 + [pl.BlockSpec(a.shape, lambda b, j: (0, 0)) for a in small],
        out_specs=qspec(A_WIDTH + B_WIDTH),
        out_shape=jax.ShapeDtypeStruct((bsz * seq, A_WIDTH + B_WIDTH), BF16),
        scratch_shapes=[pltpu.VMEM((seq + N_META, tq), F32) for _ in range(2)],
        compiler_params=pltpu.CompilerParams(dimension_semantics=("parallel", "arbitrary"),
                                             vmem_limit_bytes=VMEM_LIMIT_BYTES),
        name="attention",
    )(qa, ka, vat, qb, kb, vbt, *small)


def _out_ffn_call(o, h1, tm, wout, ln2, ffn_w, ln3):
    rows, d = h1.shape
    row = lambda w: pl.BlockSpec((tm, w), lambda i: (i, 0))
    resident = [wout, *ln2, *ffn_w, *ln3]
    return pl.pallas_call(
        _out_ffn_kernel,
        grid=(rows // tm,),
        in_specs=[row(o.shape[1]), row(d)] + [_resident(a.shape) for a in resident],
        out_specs=row(d),
        out_shape=jax.ShapeDtypeStruct((rows, d), F32),
        compiler_params=pltpu.CompilerParams(dimension_semantics=("parallel",), vmem_limit_bytes=VMEM_LIMIT_BYTES),
        name="out_ffn",
    )(o, h1, *resident)


def _ffn_weights(w1, w3, w2):
    return w1.astype(BF16), w3.astype(BF16), w2.astype(BF16)


def _rope_tables(seq):
    rows = jnp.repeat(jnp.arange(seq // GRID_W, dtype=jnp.int32), GRID_W).astype(F32)[:, None]
    cols = jnp.tile(jnp.arange(GRID_W, dtype=jnp.int32), seq // GRID_W).astype(F32)[:, None]

    def cos_sin(rot_dim):
        axis_dim = rot_dim // 2
        inv_freq = ROPE_THETA ** (-jnp.arange(0, axis_dim, 2, dtype=F32) / axis_dim)
        ang = jnp.concatenate([rows * inv_freq, cols * inv_freq], axis=-1)
        return jnp.cos(ang), jnp.sin(ang)

    cos_a, sin_a = cos_sin(A_HEAD_DIM)
    ca = jnp.tile(cos_a, (1, LANES // cos_a.shape[1]))
    sa = jnp.concatenate([-sin_a, -sin_a, sin_a, sin_a], axis=-1)
    cos_b, sin_b = cos_sin(B_ROPE_DIM)
    pad = LANES - B_QK_DIM
    cb = jnp.concatenate([jnp.ones((seq, B_NOPE_DIM), F32), cos_b, cos_b, jnp.ones((seq, pad), F32)], axis=-1)
    sb = jnp.concatenate([jnp.zeros((seq, B_NOPE_DIM), F32), -sin_b, sin_b, jnp.zeros((seq, pad), F32)], axis=-1)
    return ca, sa, cb, sb


def kernel(x, meta_tokens, ffn1_w1, ffn1_w3, ffn1_w2, ln1_g, ln1_b, w_in, q_norm_a, k_norm_a, cq_norm, ckv_norm, w_uq, w_ukv, out_norm_a, out_norm_b, w_out, ln2_g, ln2_b, ffn2_w1, ffn2_w3, ffn2_w2, ln3_g, ln3_b):
    bsz, seq, d = x.shape
    assert seq % ROW_TILE == 0 and seq % Q_TILE == 0 and seq % GRID_W == 0 and seq % KEY_CHUNK == 0
    assert ffn1_w1.shape[2] % FFN_CHUNK == 0 and d % LANES == 0
    i = 0

    ffn1 = _ffn_weights(ffn1_w1[i], ffn1_w3[i], ffn1_w2[i])
    ffn2 = _ffn_weights(ffn2_w1[i], ffn2_w3[i], ffn2_w2[i])
    row_vec = lambda v: v.reshape(1, -1).astype(F32)
    col_vec = lambda v: v.reshape(-1, 1).astype(F32)
    half = A_HEAD_DIM // 2
    pair_perm = jnp.arange(LANES).reshape(2, 2, half).transpose(1, 0, 2).reshape(LANES)
    qk_perm = (jnp.arange(OFF_VA).reshape(-1, LANES)[:, pair_perm]).reshape(-1)
    win = w_in[i]
    win = jnp.concatenate([win[:, qk_perm], win[:, OFF_VA:]], axis=1)
    n_main = OFF_KPE
    kpe_cols = jnp.pad(win[:, n_main:], ((0, 0), (B_NOPE_DIM, LANES - B_QK_DIM)))
    win_p = jnp.concatenate([win[:, :n_main], kpe_cols], axis=1).astype(BF16)
    wuq_p = jnp.pad(w_uq[i].reshape(B_Q_RANK, B_HEADS, B_QK_DIM),
                    ((0, 0), (0, 0), (0, LANES - B_QK_DIM))).reshape(B_Q_RANK, B_HEADS * LANES).astype(BF16)
    rope_lo, rope_mid = B_NOPE_DIM, B_NOPE_DIM + B_ROPE_DIM // 2
    wuq3 = wuq_p.reshape(B_Q_RANK, B_HEADS, LANES)
    zero_cols = lambda n: jnp.zeros((B_Q_RANK, B_HEADS, n), BF16)
    wuqr_p = jnp.concatenate([zero_cols(rope_lo), wuq3[:, :, rope_mid:B_QK_DIM], wuq3[:, :, rope_lo:rope_mid],
                              zero_cols(LANES - B_QK_DIM)], axis=2).reshape(B_Q_RANK, B_HEADS * LANES)
    wukv = w_ukv[i].reshape(B_KV_RANK, B_HEADS, B_NOPE_DIM + B_V_DIM)
    wuk_p = jnp.pad(wukv[:, :, :B_NOPE_DIM],
                    ((0, 0), (0, 0), (0, LANES - B_NOPE_DIM))).reshape(B_KV_RANK, B_HEADS * LANES).astype(BF16)
    wuv = wukv[:, :, B_NOPE_DIM:].reshape(B_KV_RANK, B_WIDTH).astype(BF16)
    pair_gain = lambda g: jnp.tile(g, LANES // A_HEAD_DIM)[pair_perm].reshape(1, LANES).astype(F32)
    gq = pair_gain(q_norm_a[i] * (A_HEAD_DIM ** -0.5 * LOG2_E))
    gk = pair_gain(k_norm_a[i])
    gains = (gq, gk, row_vec(cq_norm[i]), row_vec(ckv_norm[i]))
    ln1 = (row_vec(ln1_g[i]), row_vec(ln1_b[i]))
    wu = (wuq_p, wuqr_p, wuk_p, wuv)

    tables = _rope_tables(seq)
    ones, zeros = jnp.ones((N_META, LANES), F32), jnp.zeros((N_META, LANES), F32)
    meta_tables = (ones, zeros, ones, zeros)

    x2d = x.reshape(bsz * seq, d)
    h1, qa, ka, vat, qb, kb, vbt = _ffn_proj_call(x2d, seq, ROW_TILE, ffn1, ln1, win_p, gains, wu, tables)
    _, _, kam, vamt, _, kbm, vbmt = _ffn_proj_call(meta_tokens.astype(F32), N_META, N_META, ffn1, ln1, win_p, gains,
                                                   wu, meta_tables)

    o = _attention_call(bsz, seq, Q_TILE, qa, ka, vat, qb, kb, vbt, (kam, vamt, kbm, vbmt),
                        col_vec(out_norm_a[i]), col_vec(out_norm_b[i]))

    out = _out_ffn_call(o, h1, ROW_TILE, w_out[i].astype(BF16), (row_vec(ln2_g[i]), row_vec(ln2_b[i])),
                        ffn2, (row_vec(ln3_g[i]), row_vec(ln3_b[i])))
    return out.reshape(bsz, seq, d)
```

```python
import math

import jax
import jax.numpy as jnp
from jax import lax
from jax.experimental import pallas as pl
from jax.experimental.pallas import tpu as pltpu

F32 = jnp.float32
BF16 = jnp.bfloat16

LANES = 128
BF16_SUBLANES = 16
VMEM_LIMIT_BYTES = 56 * 1024 * 1024

N_META = 16
GRID_W = 64
ROPE_THETA = 10000.0
RMS_EPS = 1e-6
LN_EPS = 1e-5
A_HEADS, A_KV_HEADS, A_HEAD_DIM = 8, 2, 64
A_GROUP = A_HEADS // A_KV_HEADS
B_HEADS, B_Q_RANK, B_KV_RANK = 8, 256, 128
B_NOPE_DIM, B_ROPE_DIM, B_V_DIM = 64, 32, 64
B_QK_DIM = B_NOPE_DIM + B_ROPE_DIM
A_WIDTH = A_HEADS * A_HEAD_DIM
B_WIDTH = B_HEADS * B_V_DIM
DEPTH = 1
ALPHA = (2.0 * DEPTH) ** 0.25
LOG2_E = math.log2(math.e)

OFF_QA, OFF_KA, OFF_VA = 0, A_WIDTH, A_WIDTH + LANES
OFF_CQ = OFF_VA + LANES
OFF_CKV = OFF_CQ + B_Q_RANK
OFF_KPE = OFF_CKV + B_KV_RANK
IN_PAD_WIDTH = OFF_KPE + LANES

FFN_CHUNK = 256
ROW_TILE = 512
OUT_ROW_TILE = 1024
Q_TILE = 512
KEY_CHUNK = 512


def _layer_norm(z, g, b):
    mu = jnp.mean(z, axis=-1, keepdims=True)
    zc = z - mu
    var = jnp.mean(zc * zc, axis=-1, keepdims=True)
    return zc * lax.rsqrt(var + LN_EPS) * g + b


def _swiglu_chunk(xb, w1_ref, w3_ref, w2_ref, c):
    cols = slice(c * FFN_CHUNK, (c + 1) * FFN_CHUNK)
    u = jnp.dot(xb, w1_ref[:, cols], preferred_element_type=F32)
    g = jnp.dot(xb, w3_ref[:, cols], preferred_element_type=F32)
    a = (u * jax.nn.sigmoid(u) * g).astype(BF16)
    return jnp.dot(a, w2_ref[cols, :], preferred_element_type=F32)


def _swiglu(xb, w1_ref, w3_ref, w2_ref):
    acc = _swiglu_chunk(xb, w1_ref, w3_ref, w2_ref, 0)
    for c in range(1, w1_ref.shape[1] // FFN_CHUNK):
        acc = acc + _swiglu_chunk(xb, w1_ref, w3_ref, w2_ref, c)
    return acc


def _rot_b(x, lane):
    half = B_ROPE_DIM // 2
    return jnp.where(lane < B_NOPE_DIM + half, pltpu.roll(x, LANES - half, 1), pltpu.roll(x, half, 1))


def _ffn_proj_kernel(x_ref, w1_ref, w3_ref, w2_ref, lng_ref, lnb_ref, win_ref,
                     gq_ref, gk_ref, gcq_ref, gckv_ref, wuq_ref, wuqr_ref, wuk_ref, wuv_ref,
                     ca_ref, sa_ref, cb_ref, sb_ref,
                     h1_ref, qa_ref, ka_ref, vat_ref, qb_ref, kb_ref, vbt_ref):
    tm = x_ref.shape[0]
    x = x_ref[...]
    y = _swiglu(x.astype(BF16), w1_ref, w3_ref, w2_ref)
    h1 = _layer_norm(ALPHA * x + 0.5 * y, lng_ref[...], lnb_ref[...])
    h1_ref[...] = h1

    proj = jnp.dot(h1.astype(BF16), win_ref[...], preferred_element_type=F32)

    n_pos_blocks = ca_ref.shape[0] // tm
    row0 = pl.multiple_of((pl.program_id(0) % n_pos_blocks) * tm, tm)
    ca = ca_ref[pl.ds(row0, tm), :]
    sa = sa_ref[pl.ds(row0, tm), :]
    cb = cb_ref[pl.ds(row0, tm), :]
    sb = sb_ref[pl.ds(row0, tm), :]

    lane = lax.broadcasted_iota(jnp.int32, (tm, LANES), 1)
    first = (lane % A_HEAD_DIM) < A_HEAD_DIM // 2

    def norm_rope_a(xc, g):
        x2 = xc * xc
        s0 = jnp.sum(jnp.where(first, x2, 0.0), axis=-1, keepdims=True)
        s1 = jnp.sum(jnp.where(first, 0.0, x2), axis=-1, keepdims=True)
        ms = jnp.where(first, s0, s1) * (1.0 / A_HEAD_DIM)
        n = xc * lax.rsqrt(ms + RMS_EPS) * g
        return n * ca + pltpu.roll(n, LANES // 2, 1) * sa

    gq = gq_ref[...]
    for c in range(A_WIDTH // LANES):
        qa_ref[:, c * LANES:(c + 1) * LANES] = norm_rope_a(
            proj[:, OFF_QA + c * LANES:OFF_QA + (c + 1) * LANES], gq).astype(BF16)
    ka_ref[...] = norm_rope_a(proj[:, OFF_KA:OFF_KA + LANES], gk_ref[...]).astype(BF16)
    vat_ref[...] = proj[:, OFF_VA:OFF_VA + LANES].T.astype(BF16)

    def rms(v, g):
        return v * lax.rsqrt(jnp.mean(v * v, axis=-1, keepdims=True) + RMS_EPS) * g

    cqn = rms(proj[:, OFF_CQ:OFF_CQ + B_Q_RANK], gcq_ref[...]).astype(BF16)
    ckvn = rms(proj[:, OFF_CKV:OFF_CKV + B_KV_RANK], gckv_ref[...]).astype(BF16)
    qb = jnp.dot(cqn, wuq_ref[...], preferred_element_type=F32)
    qb_rot = jnp.dot(cqn, wuqr_ref[...], preferred_element_type=F32)
    kn = jnp.dot(ckvn, wuk_ref[...], preferred_element_type=F32)
    vbt_ref[...] = jnp.dot(ckvn, wuv_ref[...], preferred_element_type=F32).T.astype(BF16)

    kpe = proj[:, OFF_KPE:OFF_KPE + LANES]
    kpe = kpe * cb + _rot_b(kpe, lane) * sb
    q_scale = B_QK_DIM ** -0.5 * LOG2_E
    for h in range(B_HEADS):
        xh = qb[:, h * LANES:(h + 1) * LANES]
        qb_ref[:, h * LANES:(h + 1) * LANES] = (
            (xh * cb + qb_rot[:, h * LANES:(h + 1) * LANES] * sb) * q_scale).astype(BF16)
        kb_ref[:, h * LANES:(h + 1) * LANES] = (kn[:, h * LANES:(h + 1) * LANES] + kpe).astype(BF16)


def _attention_kernel(qa_ref, ka_ref, vat_ref, qb_ref, kb_ref, vbt_ref,
                      kam_ref, vamt_ref, kbm_ref, vbmt_ref, ga_ref, gb_ref, o_ref, s0_ref, s1_ref):
    tq = qa_ref.shape[0]
    n_keys = ka_ref.shape[0]
    n_chunks = n_keys // KEY_CHUNK
    hd = A_HEAD_DIM
    s_refs = (s0_ref, s1_ref)
    lane = lax.broadcasted_iota(jnp.int32, (tq, LANES), 1)
    first = (lane % A_HEAD_DIM) < A_HEAD_DIM // 2
    ones = jnp.ones((BF16_SUBLANES, KEY_CHUNK), BF16)
    zero = jnp.minimum(pl.program_id(1), 0)
    meta_rows = pl.ds(pl.multiple_of(zero + n_keys, N_META), N_META)

    def chunk_rows(j):
        return pl.ds(pl.multiple_of(zero + j * KEY_CHUNK, KEY_CHUNK), KEY_CHUNK)

    def q_a(h):
        c, g = h // 2, h // A_GROUP
        half = A_HEAD_DIM // 2
        qc = qa_ref[:, c * LANES:(c + 1) * LANES].astype(F32)
        if (h % 2) == g:
            on_kv_lanes = qc
        else:
            on_kv_lanes = pltpu.roll(qc, half if g == 1 else LANES - half, 1)
        return jnp.where(first if g == 0 else jnp.logical_not(first), on_kv_lanes, 0.0)

    heads = []
    for h in range(A_HEADS):
        g = h // A_GROUP
        heads.append((lambda h=h: q_a(h), lambda: ka_ref[...],
                      lambda g=g: vat_ref[g * hd:(g + 1) * hd, :], lambda: kam_ref[...],
                      lambda g=g: vamt_ref[g * hd:(g + 1) * hd, :]))
    for h in range(B_HEADS):
        heads.append((lambda h=h: qb_ref[:, h * LANES:(h + 1) * LANES].astype(F32),
                      lambda h=h: kb_ref[:, h * LANES:(h + 1) * LANES],
                      lambda h=h: vbt_ref[h * hd:(h + 1) * hd, :],
                      lambda h=h: kbm_ref[:, h * LANES:(h + 1) * LANES],
                      lambda h=h: vbmt_ref[h * hd:(h + 1) * hd, :]))

    def scores_begin(i):
        qt = heads[i][0]().T.astype(BF16)
        smt = jnp.dot(heads[i][3](), qt, preferred_element_type=F32)
        s_refs[i % 2][meta_rows, :] = smt
        return qt, jnp.max(smt, axis=0, keepdims=True)

    def scores_chunk(i, j, qt, m):
        k = heads[i][1]()
        st = jnp.dot(k[j * KEY_CHUNK:(j + 1) * KEY_CHUNK, :], qt, preferred_element_type=F32)
        s_refs[i % 2][chunk_rows(j), :] = st
        return jnp.maximum(m, jnp.max(st, axis=0, keepdims=True))

    def values_begin(i, m):
        pm = jnp.exp2(s_refs[i % 2][meta_rows, :] - m)
        lhs = jnp.concatenate([heads[i][4](), ones[:, :N_META]], axis=0)
        return jnp.dot(lhs, pm.astype(BF16), preferred_element_type=F32)

    def values_chunk(i, j, m, ot):
        p = jnp.exp2(s_refs[i % 2][chunk_rows(j), :] - m)
        lhs = jnp.concatenate([heads[i][2]()[:, j * KEY_CHUNK:(j + 1) * KEY_CHUNK], ones], axis=0)
        return ot + jnp.dot(lhs, p.astype(BF16), preferred_element_type=F32)

    outs = []
    qt, m_next = scores_begin(0)
    for j in range(n_chunks):
        m_next = scores_chunk(0, j, qt, m_next)
    for i in range(len(heads)):
        m = m_next
        more = i + 1 < len(heads)
        if more:
            qt, m_next = scores_begin(i + 1)
        ot = values_begin(i, m)
        for j in range(n_chunks):
            if more:
                m_next = scores_chunk(i + 1, j, qt, m_next)
            ot = values_chunk(i, j, m, ot)
        outs.append(ot[:hd] * (1.0 / ot[hd:hd + 1]))

    def group_rms(head_rows, g):
        xt = jnp.concatenate(head_rows, axis=0)
        r = lax.rsqrt(jnp.mean(xt * xt, axis=0, keepdims=True) + RMS_EPS)
        return (xt * r * g).T.astype(BF16)

    o_ref[:, :A_WIDTH] = group_rms(outs[:A_HEADS], ga_ref[...])
    o_ref[:, A_WIDTH:] = group_rms(outs[A_HEADS:], gb_ref[...])


def _out_ffn_kernel(o_ref, h1_ref, wout_ref, g2_ref, b2_ref, w1_ref, w3_ref, w2_ref, g3_ref, b3_ref,
                    out_ref):
    half = o_ref.shape[0] // 2
    ra, rb = slice(0, half), slice(half, 2 * half)
    n_chunks = w1_ref.shape[1] // FFN_CHUNK
    ffn = (w1_ref, w3_ref, w2_ref)
    ln2 = lambda rows, mix: _layer_norm(ALPHA * h1_ref[rows, :] + mix, g2_ref[...], b2_ref[...])
    ln3 = lambda h2, y: _layer_norm(ALPHA * h2 + 0.5 * y, g3_ref[...], b3_ref[...])

    mix_a = jnp.dot(o_ref[ra, :], wout_ref[...], preferred_element_type=F32)
    mix_b = jnp.dot(o_ref[rb, :], wout_ref[...], preferred_element_type=F32)
    h2a = ln2(ra, mix_a)
    xa = h2a.astype(BF16)
    ya = _swiglu_chunk(xa, *ffn, 0)
    h2b = ln2(rb, mix_b)
    xb = h2b.astype(BF16)
    for c in range(1, n_chunks):
        ya = ya + _swiglu_chunk(xa, *ffn, c)
    yb = _swiglu_chunk(xb, *ffn, 0)
    out_ref[ra, :] = ln3(h2a, ya)
    for c in range(1, n_chunks):
        yb = yb + _swiglu_chunk(xb, *ffn, c)
    out_ref[rb, :] = ln3(h2b, yb)


def _resident(shape):
    return pl.BlockSpec(shape, lambda *_: (0,) * len(shape), pipeline_mode=pl.Buffered(1))


def _ffn_proj_call(x2d, seq, tm, ffn_w, ln, win, gains, wu, tables):
    rows, d = x2d.shape
    n_pos_blocks = seq // tm
    row = lambda w: pl.BlockSpec((tm, w), lambda i: (i, 0))
    col = lambda w: pl.BlockSpec((w, tm), lambda i: (i // n_pos_blocks, i % n_pos_blocks))
    resident = [*ffn_w, *ln, win, *gains, *wu, *tables]
    tok = lambda w, dt=BF16: jax.ShapeDtypeStruct((rows, w), dt)
    feat = lambda w: jax.ShapeDtypeStruct((rows // seq * w, seq), BF16)
    return pl.pallas_call(
        _ffn_proj_kernel,
        grid=(rows // tm,),
        in_specs=[row(d)] + [_resident(a.shape) for a in resident],
        out_specs=[row(d), row(A_WIDTH), row(LANES), col(LANES), row(B_HEADS * LANES), row(B_HEADS * LANES),
                   col(B_WIDTH)],
        out_shape=[tok(d, F32), tok(A_WIDTH), tok(LANES), feat(LANES), tok(B_HEADS * LANES), tok(B_HEADS * LANES),
                   feat(B_WIDTH)],
        compiler_params=pltpu.CompilerParams(dimension_semantics=("parallel",), vmem_limit_bytes=VMEM_LIMIT_BYTES),
        name="ffn_proj",
    )(x2d, *resident)


def _attention_call(bsz, seq, tq, qa, ka, vat, qb, kb, vbt, meta_kv, ga, gb):
    nq = seq // tq
    qspec = lambda w: pl.BlockSpec((tq, w), lambda b, j: (b * nq + j, 0))
    kspec = lambda w: pl.BlockSpec((seq, w), lambda b, j: (b, 0))
    vspec = lambda w: pl.BlockSpec((w, seq), lambda b, j: (b, 0))
    small = [*meta_kv, ga, gb]
    return pl.pallas_call(
        _attention_kernel,
        grid=(bsz, nq),
        in_specs=[qspec(qa.shape[1]), kspec(ka.shape[1]), vspec(LANES),
                  qspec(qb.shape[1]), kspec(kb.shape[1]), vspec(B_WIDTH)]
                 + [pl.BlockSpec(a.shape, lambda b, j: (0, 0)) for a in small],
        out_specs=qspec(A_WIDTH + B_WIDTH),
        out_shape=jax.ShapeDtypeStruct((bsz * seq, A_WIDTH + B_WIDTH), BF16),
        scratch_shapes=[pltpu.VMEM((seq + N_META, tq), F32) for _ in range(2)],
        compiler_params=pltpu.CompilerParams(dimension_semantics=("parallel", "arbitrary"),
                                             vmem_limit_bytes=VMEM_LIMIT_BYTES),
        name="attention",
    )(qa, ka, vat, qb, kb, vbt, *small)


def _out_ffn_call(o, h1, tm, wout, ln2, ffn_w, ln3):
    rows, d = h1.shape
    row = lambda w: pl.BlockSpec((tm, w), lambda i: (i, 0))
    resident = [wout, *ln2, *ffn_w, *ln3]
    return pl.pallas_call(
        _out_ffn_kernel,
        grid=(rows // tm,),
        in_specs=[row(o.shape[1]), row(d)] + [_resident(a.shape) for a in resident],
        out_specs=row(d),
        out_shape=jax.ShapeDtypeStruct((rows, d), F32),
        compiler_params=pltpu.CompilerParams(dimension_semantics=("parallel",), vmem_limit_bytes=VMEM_LIMIT_BYTES),
        name="out_ffn",
    )(o, h1, *resident)


def _ffn_weights(w1, w3, w2):
    return w1.astype(BF16), w3.astype(BF16), w2.astype(BF16)


def _rope_tables(seq):
    rows = jnp.repeat(jnp.arange(seq // GRID_W, dtype=jnp.int32), GRID_W).astype(F32)[:, None]
    cols = jnp.tile(jnp.arange(GRID_W, dtype=jnp.int32), seq // GRID_W).astype(F32)[:, None]

    def cos_sin(rot_dim):
        axis_dim = rot_dim // 2
        inv_freq = ROPE_THETA ** (-jnp.arange(0, axis_dim, 2, dtype=F32) / axis_dim)
        ang = jnp.concatenate([rows * inv_freq, cols * inv_freq], axis=-1)
        return jnp.cos(ang), jnp.sin(ang)

    cos_a, sin_a = cos_sin(A_HEAD_DIM)
    ca = jnp.tile(cos_a, (1, LANES // cos_a.shape[1]))
    sa = jnp.concatenate([-sin_a, -sin_a, sin_a, sin_a], axis=-1)
    cos_b, sin_b = cos_sin(B_ROPE_DIM)
    pad = LANES - B_QK_DIM
    cb = jnp.concatenate([jnp.ones((seq, B_NOPE_DIM), F32), cos_b, cos_b, jnp.ones((seq, pad), F32)], axis=-1)
    sb = jnp.concatenate([jnp.zeros((seq, B_NOPE_DIM), F32), -sin_b, sin_b, jnp.zeros((seq, pad), F32)], axis=-1)
    return ca, sa, cb, sb


def kernel(x, meta_tokens, ffn1_w1, ffn1_w3, ffn1_w2, ln1_g, ln1_b, w_in, q_norm_a, k_norm_a, cq_norm, ckv_norm, w_uq, w_ukv, out_norm_a, out_norm_b, w_out, ln2_g, ln2_b, ffn2_w1, ffn2_w3, ffn2_w2, ln3_g, ln3_b):
    bsz, seq, d = x.shape
    assert seq % ROW_TILE == 0 and (bsz * seq) % OUT_ROW_TILE == 0 and seq % Q_TILE == 0 and seq % GRID_W == 0 and seq % KEY_CHUNK == 0
    assert ffn1_w1.shape[2] % FFN_CHUNK == 0 and d % LANES == 0
    i = 0

    ffn1 = _ffn_weights(ffn1_w1[i], ffn1_w3[i], ffn1_w2[i])
    ffn2 = _ffn_weights(ffn2_w1[i], ffn2_w3[i], ffn2_w2[i])
    row_vec = lambda v: v.reshape(1, -1).astype(F32)
    col_vec = lambda v: v.reshape(-1, 1).astype(F32)
    half = A_HEAD_DIM // 2
    pair_perm = jnp.arange(LANES).reshape(2, 2, half).transpose(1, 0, 2).reshape(LANES)
    qk_perm = (jnp.arange(OFF_VA).reshape(-1, LANES)[:, pair_perm]).reshape(-1)
    win = w_in[i]
    win = jnp.concatenate([win[:, qk_perm], win[:, OFF_VA:]], axis=1)
    n_main = OFF_KPE
    kpe_cols = jnp.pad(win[:, n_main:], ((0, 0), (B_NOPE_DIM, LANES - B_QK_DIM)))
    win_p = jnp.concatenate([win[:, :n_main], kpe_cols], axis=1).astype(BF16)
    wuq_p = jnp.pad(w_uq[i].reshape(B_Q_RANK, B_HEADS, B_QK_DIM),
                    ((0, 0), (0, 0), (0, LANES - B_QK_DIM))).reshape(B_Q_RANK, B_HEADS * LANES).astype(BF16)
    rope_lo, rope_mid = B_NOPE_DIM, B_NOPE_DIM + B_ROPE_DIM // 2
    wuq3 = wuq_p.reshape(B_Q_RANK, B_HEADS, LANES)
    zero_cols = lambda n: jnp.zeros((B_Q_RANK, B_HEADS, n), BF16)
    wuqr_p = jnp.concatenate([zero_cols(rope_lo), wuq3[:, :, rope_mid:B_QK_DIM], wuq3[:, :, rope_lo:rope_mid],
                              zero_cols(LANES - B_QK_DIM)], axis=2).reshape(B_Q_RANK, B_HEADS * LANES)
    wukv = w_ukv[i].reshape(B_KV_RANK, B_HEADS, B_NOPE_DIM + B_V_DIM)
    wuk_p = jnp.pad(wukv[:, :, :B_NOPE_DIM],
                    ((0, 0), (0, 0), (0, LANES - B_NOPE_DIM))).reshape(B_KV_RANK, B_HEADS * LANES).astype(BF16)
    wuv = wukv[:, :, B_NOPE_DIM:].reshape(B_KV_RANK, B_WIDTH).astype(BF16)
    pair_gain = lambda g: jnp.tile(g, LANES // A_HEAD_DIM)[pair_perm].reshape(1, LANES).astype(F32)
    gq = pair_gain(q_norm_a[i] * (A_HEAD_DIM ** -0.5 * LOG2_E))
    gk = pair_gain(k_norm_a[i])
    gains = (gq, gk, row_vec(cq_norm[i]), row_vec(ckv_norm[i]))
    ln1 = (row_vec(ln1_g[i]), row_vec(ln1_b[i]))
    wu = (wuq_p, wuqr_p, wuk_p, wuv)

    tables = _rope_tables(seq)
    ones, zeros = jnp.ones((N_META, LANES), F32), jnp.zeros((N_META, LANES), F32)
    meta_tables = (ones, zeros, ones, zeros)

    x2d = x.reshape(bsz * seq, d)
    h1, qa, ka, vat, qb, kb, vbt = _ffn_proj_call(x2d, seq, ROW_TILE, ffn1, ln1, win_p, gains, wu, tables)
    _, _, kam, vamt, _, kbm, vbmt = _ffn_proj_call(meta_tokens.astype(F32), N_META, N_META, ffn1, ln1, win_p, gains,
                                                   wu, meta_tables)

    o = _attention_call(bsz, seq, Q_TILE, qa, ka, vat, qb, kb, vbt, (kam, vamt, kbm, vbmt),
                        col_vec(out_norm_a[i]), col_vec(out_norm_b[i]))

    out = _out_ffn_call(o, h1, OUT_ROW_TILE, w_out[i].astype(BF16), (row_vec(ln2_g[i]), row_vec(ln2_b[i])),
                        ffn2, (row_vec(ln3_g[i]), row_vec(ln3_b[i])))
    return out.reshape(bsz, seq, d)
```

```python
import math

import jax
import jax.numpy as jnp
from jax import lax
from jax.experimental import pallas as pl
from jax.experimental.pallas import tpu as pltpu

F32 = jnp.float32
BF16 = jnp.bfloat16

LANES = 128
BF16_SUBLANES = 16
VMEM_LIMIT_BYTES = 56 * 1024 * 1024

N_META = 16
GRID_W = 64
ROPE_THETA = 10000.0
RMS_EPS = 1e-6
LN_EPS = 1e-5
A_HEADS, A_KV_HEADS, A_HEAD_DIM = 8, 2, 64
A_GROUP = A_HEADS // A_KV_HEADS
B_HEADS, B_Q_RANK, B_KV_RANK = 8, 256, 128
B_NOPE_DIM, B_ROPE_DIM, B_V_DIM = 64, 32, 64
B_QK_DIM = B_NOPE_DIM + B_ROPE_DIM
A_WIDTH = A_HEADS * A_HEAD_DIM
B_WIDTH = B_HEADS * B_V_DIM
DEPTH = 1
ALPHA = (2.0 * DEPTH) ** 0.25
LOG2_E = math.log2(math.e)

OFF_QA, OFF_KA, OFF_VA = 0, A_WIDTH, A_WIDTH + LANES
OFF_CQ = OFF_VA + LANES
OFF_CKV = OFF_CQ + B_Q_RANK
OFF_KPE = OFF_CKV + B_KV_RANK
IN_PAD_WIDTH = OFF_KPE + LANES

FFN_CHUNK = 256
ROW_TILE = 512
OUT_ROW_TILE = 1024
Q_TILE = 1024
Q_SUB = 512
KEY_CHUNK = 512


def _layer_norm(z, g, b):
    mu = jnp.mean(z, axis=-1, keepdims=True)
    zc = z - mu
    var = jnp.mean(zc * zc, axis=-1, keepdims=True)
    return zc * lax.rsqrt(var + LN_EPS) * g + b


def _swiglu_chunk(xb, w1_ref, w3_ref, w2_ref, c):
    cols = slice(c * FFN_CHUNK, (c + 1) * FFN_CHUNK)
    u = jnp.dot(xb, w1_ref[:, cols], preferred_element_type=F32)
    g = jnp.dot(xb, w3_ref[:, cols], preferred_element_type=F32)
    a = (u * jax.nn.sigmoid(u) * g).astype(BF16)
    return jnp.dot(a, w2_ref[cols, :], preferred_element_type=F32)


def _swiglu(xb, w1_ref, w3_ref, w2_ref):
    acc = _swiglu_chunk(xb, w1_ref, w3_ref, w2_ref, 0)
    for c in range(1, w1_ref.shape[1] // FFN_CHUNK):
        acc = acc + _swiglu_chunk(xb, w1_ref, w3_ref, w2_ref, c)
    return acc


def _rot_b(x, lane):
    half = B_ROPE_DIM // 2
    return jnp.where(lane < B_NOPE_DIM + half, pltpu.roll(x, LANES - half, 1), pltpu.roll(x, half, 1))


def _ffn_proj_kernel(x_ref, w1_ref, w3_ref, w2_ref, lng_ref, lnb_ref, win_ref,
                     gq_ref, gk_ref, gcq_ref, gckv_ref, wuq_ref, wuqr_ref, wuk_ref, wuv_ref,
                     ca_ref, sa_ref, cb_ref, sb_ref,
                     h1_ref, qa_ref, ka_ref, vat_ref, qb_ref, kb_ref, vbt_ref):
    tm = x_ref.shape[0]
    x = x_ref[...]
    y = _swiglu(x.astype(BF16), w1_ref, w3_ref, w2_ref)
    h1 = _layer_norm(ALPHA * x + 0.5 * y, lng_ref[...], lnb_ref[...])
    h1_ref[...] = h1

    proj = jnp.dot(h1.astype(BF16), win_ref[...], preferred_element_type=F32)

    n_pos_blocks = ca_ref.shape[0] // tm
    row0 = pl.multiple_of((pl.program_id(0) % n_pos_blocks) * tm, tm)
    ca = ca_ref[pl.ds(row0, tm), :]
    sa = sa_ref[pl.ds(row0, tm), :]
    cb = cb_ref[pl.ds(row0, tm), :]
    sb = sb_ref[pl.ds(row0, tm), :]

    lane = lax.broadcasted_iota(jnp.int32, (tm, LANES), 1)
    first = (lane % A_HEAD_DIM) < A_HEAD_DIM // 2

    def norm_rope_a(xc, g):
        x2 = xc * xc
        s0 = jnp.sum(jnp.where(first, x2, 0.0), axis=-1, keepdims=True)
        s1 = jnp.sum(jnp.where(first, 0.0, x2), axis=-1, keepdims=True)
        ms = jnp.where(first, s0, s1) * (1.0 / A_HEAD_DIM)
        n = xc * lax.rsqrt(ms + RMS_EPS) * g
        return n * ca + pltpu.roll(n, LANES // 2, 1) * sa

    gq = gq_ref[...]
    for c in range(A_WIDTH // LANES):
        qa_ref[:, c * LANES:(c + 1) * LANES] = norm_rope_a(
            proj[:, OFF_QA + c * LANES:OFF_QA + (c + 1) * LANES], gq).astype(BF16)
    ka_ref[...] = norm_rope_a(proj[:, OFF_KA:OFF_KA + LANES], gk_ref[...]).astype(BF16)
    vat_ref[...] = proj[:, OFF_VA:OFF_VA + LANES].T.astype(BF16)

    def rms(v, g):
        return v * lax.rsqrt(jnp.mean(v * v, axis=-1, keepdims=True) + RMS_EPS) * g

    cqn = rms(proj[:, OFF_CQ:OFF_CQ + B_Q_RANK], gcq_ref[...]).astype(BF16)
    ckvn = rms(proj[:, OFF_CKV:OFF_CKV + B_KV_RANK], gckv_ref[...]).astype(BF16)
    qb = jnp.dot(cqn, wuq_ref[...], preferred_element_type=F32)
    qb_rot = jnp.dot(cqn, wuqr_ref[...], preferred_element_type=F32)
    kn = jnp.dot(ckvn, wuk_ref[...], preferred_element_type=F32)
    vbt_ref[...] = jnp.dot(ckvn, wuv_ref[...], preferred_element_type=F32).T.astype(BF16)

    kpe = proj[:, OFF_KPE:OFF_KPE + LANES]
    kpe = kpe * cb + _rot_b(kpe, lane) * sb
    q_scale = B_QK_DIM ** -0.5 * LOG2_E
    for h in range(B_HEADS):
        xh = qb[:, h * LANES:(h + 1) * LANES]
        qb_ref[:, h * LANES:(h + 1) * LANES] = (
            (xh * cb + qb_rot[:, h * LANES:(h + 1) * LANES] * sb) * q_scale).astype(BF16)
        kb_ref[:, h * LANES:(h + 1) * LANES] = (kn[:, h * LANES:(h + 1) * LANES] + kpe).astype(BF16)


def _attention_kernel(qa_ref, ka_ref, vat_ref, qb_ref, kb_ref, vbt_ref,
                      kam_ref, vamt_ref, kbm_ref, vbmt_ref, ga_ref, gb_ref, o_ref, s0_ref, s1_ref):
    tq = Q_SUB
    n_sub = qa_ref.shape[0] // tq
    n_keys = ka_ref.shape[0]
    n_chunks = n_keys // KEY_CHUNK
    hd = A_HEAD_DIM
    s_refs = (s0_ref, s1_ref)
    lane = lax.broadcasted_iota(jnp.int32, (tq, LANES), 1)
    first = (lane % A_HEAD_DIM) < A_HEAD_DIM // 2
    ones = jnp.ones((BF16_SUBLANES, KEY_CHUNK), BF16)
    zero = jnp.minimum(pl.program_id(1), 0)
    meta_rows = pl.ds(pl.multiple_of(zero + n_keys, N_META), N_META)

    def chunk_rows(j):
        return pl.ds(pl.multiple_of(zero + j * KEY_CHUNK, KEY_CHUNK), KEY_CHUNK)

    def q_a(h, t):
        c, g = h // 2, h // A_GROUP
        half = A_HEAD_DIM // 2
        qc = qa_ref[t * tq:(t + 1) * tq, c * LANES:(c + 1) * LANES].astype(F32)
        if (h % 2) == g:
            on_kv_lanes = qc
        else:
            on_kv_lanes = pltpu.roll(qc, half if g == 1 else LANES - half, 1)
        return jnp.where(first if g == 0 else jnp.logical_not(first), on_kv_lanes, 0.0)

    heads = []
    for h in range(A_HEADS):
        g = h // A_GROUP
        heads.append((lambda t, h=h: q_a(h, t), lambda: ka_ref[...],
                      lambda g=g: vat_ref[g * hd:(g + 1) * hd, :], lambda: kam_ref[...],
                      lambda g=g: vamt_ref[g * hd:(g + 1) * hd, :]))
    for h in range(B_HEADS):
        heads.append((lambda t, h=h: qb_ref[t * tq:(t + 1) * tq, h * LANES:(h + 1) * LANES].astype(F32),
                      lambda h=h: kb_ref[:, h * LANES:(h + 1) * LANES],
                      lambda h=h: vbt_ref[h * hd:(h + 1) * hd, :],
                      lambda h=h: kbm_ref[:, h * LANES:(h + 1) * LANES],
                      lambda h=h: vbmt_ref[h * hd:(h + 1) * hd, :]))

    items = [(t, h) for t in range(n_sub) for h in range(len(heads))]

    def scores_begin(n):
        t, h = items[n]
        qt = heads[h][0](t).T.astype(BF16)
        smt = jnp.dot(heads[h][3](), qt, preferred_element_type=F32)
        s_refs[n % 2][meta_rows, :] = smt
        return qt, jnp.max(smt, axis=0, keepdims=True)

    def scores_chunk(n, j, qt, m):
        k = heads[items[n][1]][1]()
        st = jnp.dot(k[j * KEY_CHUNK:(j + 1) * KEY_CHUNK, :], qt, preferred_element_type=F32)
        s_refs[n % 2][chunk_rows(j), :] = st
        return jnp.maximum(m, jnp.max(st, axis=0, keepdims=True))

    def values_begin(n, m):
        pm = jnp.exp2(s_refs[n % 2][meta_rows, :] - m)
        lhs = jnp.concatenate([heads[items[n][1]][4](), ones[:, :N_META]], axis=0)
        return jnp.dot(lhs, pm.astype(BF16), preferred_element_type=F32)

    def values_chunk(n, j, m, ot):
        p = jnp.exp2(s_refs[n % 2][chunk_rows(j), :] - m)
        lhs = jnp.concatenate([heads[items[n][1]][2]()[:, j * KEY_CHUNK:(j + 1) * KEY_CHUNK], ones], axis=0)
        return ot + jnp.dot(lhs, p.astype(BF16), preferred_element_type=F32)

    def group_rms(head_rows, g):
        xt = jnp.concatenate(head_rows, axis=0)
        r = lax.rsqrt(jnp.mean(xt * xt, axis=0, keepdims=True) + RMS_EPS)
        return (xt * r * g).T.astype(BF16)

    outs = []
    qt, m_next = scores_begin(0)
    for j in range(n_chunks):
        m_next = scores_chunk(0, j, qt, m_next)
    for n in range(len(items)):
        m = m_next
        more = n + 1 < len(items)
        if more:
            qt, m_next = scores_begin(n + 1)
        ot = values_begin(n, m)
        for j in range(n_chunks):
            if more:
                m_next = scores_chunk(n + 1, j, qt, m_next)
            ot = values_chunk(n, j, m, ot)
        outs.append(ot[:hd] * (1.0 / ot[hd:hd + 1]))
        if len(outs) == len(heads):
            rows = slice(items[n][0] * tq, (items[n][0] + 1) * tq)
            o_ref[rows, :A_WIDTH] = group_rms(outs[:A_HEADS], ga_ref[...])
            o_ref[rows, A_WIDTH:] = group_rms(outs[A_HEADS:], gb_ref[...])
            outs = []


def _out_ffn_kernel(o_ref, h1_ref, wout_ref, g2_ref, b2_ref, w1_ref, w3_ref, w2_ref, g3_ref, b3_ref,
                    out_ref):
    half = o_ref.shape[0] // 2
    ra, rb = slice(0, half), slice(half, 2 * half)
    n_chunks = w1_ref.shape[1] // FFN_CHUNK
    ffn = (w1_ref, w3_ref, w2_ref)
    ln2 = lambda rows, mix: _layer_norm(ALPHA * h1_ref[rows, :] + mix, g2_ref[...], b2_ref[...])
    ln3 = lambda h2, y: _layer_norm(ALPHA * h2 + 0.5 * y, g3_ref[...], b3_ref[...])

    mix_a = jnp.dot(o_ref[ra, :], wout_ref[...], preferred_element_type=F32)
    mix_b = jnp.dot(o_ref[rb, :], wout_ref[...], preferred_element_type=F32)
    h2a = ln2(ra, mix_a)
    xa = h2a.astype(BF16)
    ya = _swiglu_chunk(xa, *ffn, 0)
    h2b = ln2(rb, mix_b)
    xb = h2b.astype(BF16)
    for c in range(1, n_chunks):
        ya = ya + _swiglu_chunk(xa, *ffn, c)
    yb = _swiglu_chunk(xb, *ffn, 0)
    out_ref[ra, :] = ln3(h2a, ya)
    for c in range(1, n_chunks):
        yb = yb + _swiglu_chunk(xb, *ffn, c)
    out_ref[rb, :] = ln3(h2b, yb)


def _resident(shape):
    return pl.BlockSpec(shape, lambda *_: (0,) * len(shape), pipeline_mode=pl.Buffered(1))


def _ffn_proj_call(x2d, seq, tm, ffn_w, ln, win, gains, wu, tables):
    rows, d = x2d.shape
    n_pos_blocks = seq // tm
    row = lambda w: pl.BlockSpec((tm, w), lambda i: (i, 0))
    col = lambda w: pl.BlockSpec((w, tm), lambda i: (i // n_pos_blocks, i % n_pos_blocks))
    resident = [*ffn_w, *ln, win, *gains, *wu, *tables]
    tok = lambda w, dt=BF16: jax.ShapeDtypeStruct((rows, w), dt)
    feat = lambda w: jax.ShapeDtypeStruct((rows // seq * w, seq), BF16)
    return pl.pallas_call(
        _ffn_proj_kernel,
        grid=(rows // tm,),
        in_specs=[row(d)] + [_resident(a.shape) for a in resident],
        out_specs=[row(d), row(A_WIDTH), row(LANES), col(LANES), row(B_HEADS * LANES), row(B_HEADS * LANES),
                   col(B_WIDTH)],
        out_shape=[tok(d, F32), tok(A_WIDTH), tok(LANES), feat(LANES), tok(B_HEADS * LANES), tok(B_HEADS * LANES),
                   feat(B_WIDTH)],
        compiler_params=pltpu.CompilerParams(dimension_semantics=("parallel",), vmem_limit_bytes=VMEM_LIMIT_BYTES),
        name="ffn_proj",
    )(x2d, *resident)


def _attention_call(bsz, seq, tq, qa, ka, vat, qb, kb, vbt, meta_kv, ga, gb):
    nq = seq // tq
    qspec = lambda w: pl.BlockSpec((tq, w), lambda b, j: (b * nq + j, 0))
    kspec = lambda w: pl.BlockSpec((seq, w), lambda b, j: (b, 0))
    vspec = lambda w: pl.BlockSpec((w, seq), lambda b, j: (b, 0))
    small = [*meta_kv, ga, gb]
    return pl.pallas_call(
        _attention_kernel,
        grid=(bsz, nq),
        in_specs=[qspec(qa.shape[1]), kspec(ka.shape[1]), vspec(LANES),
                  qspec(qb.shape[1]), kspec(kb.shape[1]), vspec(B_WIDTH)]
                 + [pl.BlockSpec(a.shape, lambda b, j: (0, 0)) for a in small],
        out_specs=qspec(A_WIDTH + B_WIDTH),
        out_shape=jax.ShapeDtypeStruct((bsz * seq, A_WIDTH + B_WIDTH), BF16),
        scratch_shapes=[pltpu.VMEM((seq + N_META, Q_SUB), F32) for _ in range(2)],
        compiler_params=pltpu.CompilerParams(dimension_semantics=("parallel", "arbitrary"),
                                             vmem_limit_bytes=VMEM_LIMIT_BYTES),
        name="attention",
    )(qa, ka, vat, qb, kb, vbt, *small)


def _out_ffn_call(o, h1, tm, wout, ln2, ffn_w, ln3):
    rows, d = h1.shape
    row = lambda w: pl.BlockSpec((tm, w), lambda i: (i, 0))
    resident = [wout, *ln2, *ffn_w, *ln3]
    return pl.pallas_call(
        _out_ffn_kernel,
        grid=(rows // tm,),
        in_specs=[row(o.shape[1]), row(d)] + [_resident(a.shape) for a in resident],
        out_specs=row(d),
        out_shape=jax.ShapeDtypeStruct((rows, d), F32),
        compiler_params=pltpu.CompilerParams(dimension_semantics=("parallel",), vmem_limit_bytes=VMEM_LIMIT_BYTES),
        name="out_ffn",
    )(o, h1, *resident)


def _ffn_weights(w1, w3, w2):
    return w1.astype(BF16), w3.astype(BF16), w2.astype(BF16)


def _rope_tables(seq):
    rows = jnp.repeat(jnp.arange(seq // GRID_W, dtype=jnp.int32), GRID_W).astype(F32)[:, None]
    cols = jnp.tile(jnp.arange(GRID_W, dtype=jnp.int32), seq // GRID_W).astype(F32)[:, None]

    def cos_sin(rot_dim):
        axis_dim = rot_dim // 2
        inv_freq = ROPE_THETA ** (-jnp.arange(0, axis_dim, 2, dtype=F32) / axis_dim)
        ang = jnp.concatenate([rows * inv_freq, cols * inv_freq], axis=-1)
        return jnp.cos(ang), jnp.sin(ang)

    cos_a, sin_a = cos_sin(A_HEAD_DIM)
    ca = jnp.tile(cos_a, (1, LANES // cos_a.shape[1]))
    sa = jnp.concatenate([-sin_a, -sin_a, sin_a, sin_a], axis=-1)
    cos_b, sin_b = cos_sin(B_ROPE_DIM)
    pad = LANES - B_QK_DIM
    cb = jnp.concatenate([jnp.ones((seq, B_NOPE_DIM), F32), cos_b, cos_b, jnp.ones((seq, pad), F32)], axis=-1)
    sb = jnp.concatenate([jnp.zeros((seq, B_NOPE_DIM), F32), -sin_b, sin_b, jnp.zeros((seq, pad), F32)], axis=-1)
    return ca, sa, cb, sb


def kernel(x, meta_tokens, ffn1_w1, ffn1_w3, ffn1_w2, ln1_g, ln1_b, w_in, q_norm_a, k_norm_a, cq_norm, ckv_norm, w_uq, w_ukv, out_norm_a, out_norm_b, w_out, ln2_g, ln2_b, ffn2_w1, ffn2_w3, ffn2_w2, ln3_g, ln3_b):
    bsz, seq, d = x.shape
    assert seq % ROW_TILE == 0 and (bsz * seq) % OUT_ROW_TILE == 0 and seq % Q_TILE == 0 and seq % GRID_W == 0 and seq % KEY_CHUNK == 0
    assert ffn1_w1.shape[2] % FFN_CHUNK == 0 and d % LANES == 0 and Q_TILE % Q_SUB == 0
    i = 0

    ffn1 = _ffn_weights(ffn1_w1[i], ffn1_w3[i], ffn1_w2[i])
    ffn2 = _ffn_weights(ffn2_w1[i], ffn2_w3[i], ffn2_w2[i])
    row_vec = lambda v: v.reshape(1, -1).astype(F32)
    col_vec = lambda v: v.reshape(-1, 1).astype(F32)
    half = A_HEAD_DIM // 2
    pair_perm = jnp.arange(LANES).reshape(2, 2, half).transpose(1, 0, 2).reshape(LANES)
    qk_perm = (jnp.arange(OFF_VA).reshape(-1, LANES)[:, pair_perm]).reshape(-1)
    win = w_in[i]
    win = jnp.concatenate([win[:, qk_perm], win[:, OFF_VA:]], axis=1)
    n_main = OFF_KPE
    kpe_cols = jnp.pad(win[:, n_main:], ((0, 0), (B_NOPE_DIM, LANES - B_QK_DIM)))
    win_p = jnp.concatenate([win[:, :n_main], kpe_cols], axis=1).astype(BF16)
    wuq_p = jnp.pad(w_uq[i].reshape(B_Q_RANK, B_HEADS, B_QK_DIM),
                    ((0, 0), (0, 0), (0, LANES - B_QK_DIM))).reshape(B_Q_RANK, B_HEADS * LANES).astype(BF16)
    rope_lo, rope_mid = B_NOPE_DIM, B_NOPE_DIM + B_ROPE_DIM // 2
    wuq3 = wuq_p.reshape(B_Q_RANK, B_HEADS, LANES)
    zero_cols = lambda n: jnp.zeros((B_Q_RANK, B_HEADS, n), BF16)
    wuqr_p = jnp.concatenate([zero_cols(rope_lo), wuq3[:, :, rope_mid:B_QK_DIM], wuq3[:, :, rope_lo:rope_mid],
                              zero_cols(LANES - B_QK_DIM)], axis=2).reshape(B_Q_RANK, B_HEADS * LANES)
    wukv = w_ukv[i].reshape(B_KV_RANK, B_HEADS, B_NOPE_DIM + B_V_DIM)
    wuk_p = jnp.pad(wukv[:, :, :B_NOPE_DIM],
                    ((0, 0), (0, 0), (0, LANES - B_NOPE_DIM))).reshape(B_KV_RANK, B_HEADS * LANES).astype(BF16)
    wuv = wukv[:, :, B_NOPE_DIM:].reshape(B_KV_RANK, B_WIDTH).astype(BF16)
    pair_gain = lambda g: jnp.tile(g, LANES // A_HEAD_DIM)[pair_perm].reshape(1, LANES).astype(F32)
    gq = pair_gain(q_norm_a[i] * (A_HEAD_DIM ** -0.5 * LOG2_E))
    gk = pair_gain(k_norm_a[i])
    gains = (gq, gk, row_vec(cq_norm[i]), row_vec(ckv_norm[i]))
    ln1 = (row_vec(ln1_g[i]), row_vec(ln1_b[i]))
    wu = (wuq_p, wuqr_p, wuk_p, wuv)

    tables = _rope_tables(seq)
    ones, zeros = jnp.ones((N_META, LANES), F32), jnp.zeros((N_META, LANES), F32)
    meta_tables = (ones, zeros, ones, zeros)

    x2d = x.reshape(bsz * seq, d)
    h1, qa, ka, vat, qb, kb, vbt = _ffn_proj_call(x2d, seq, ROW_TILE, ffn1, ln1, win_p, gains, wu, tables)
    _, _, kam, vamt, _, kbm, vbmt = _ffn_proj_call(meta_tokens.astype(F32), N_META, N_META, ffn1, ln1, win_p, gains,
                                                   wu, meta_tables)

    o = _attention_call(bsz, seq, Q_TILE, qa, ka, vat, qb, kb, vbt, (kam, vamt, kbm, vbmt),
                        col_vec(out_norm_a[i]), col_vec(out_norm_b[i]))

    out = _out_ffn_call(o, h1, OUT_ROW_TILE, w_out[i].astype(BF16), (row_vec(ln2_g[i]), row_vec(ln2_b[i])),
                        ffn2, (row_vec(ln3_g[i]), row_vec(ln3_b[i])))
    return out.reshape(bsz, seq, d)
```

```python
import math

import jax
import jax.numpy as jnp
from jax import lax
from jax.experimental import pallas as pl
from jax.experimental.pallas import tpu as pltpu

F32 = jnp.float32
BF16 = jnp.bfloat16

LANES = 128
BF16_SUBLANES = 16
VMEM_LIMIT_BYTES = 56 * 1024 * 1024

N_META = 16
GRID_W = 64
ROPE_THETA = 10000.0
RMS_EPS = 1e-6
LN_EPS = 1e-5
A_HEADS, A_KV_HEADS, A_HEAD_DIM = 8, 2, 64
A_GROUP = A_HEADS // A_KV_HEADS
B_HEADS, B_Q_RANK, B_KV_RANK = 8, 256, 128
B_NOPE_DIM, B_ROPE_DIM, B_V_DIM = 64, 32, 64
B_QK_DIM = B_NOPE_DIM + B_ROPE_DIM
A_WIDTH = A_HEADS * A_HEAD_DIM
B_WIDTH = B_HEADS * B_V_DIM
DEPTH = 1
ALPHA = (2.0 * DEPTH) ** 0.25
LOG2_E = math.log2(math.e)

OFF_QA, OFF_KA, OFF_VA = 0, A_WIDTH, A_WIDTH + LANES
OFF_CQ = OFF_VA + LANES
OFF_CKV = OFF_CQ + B_Q_RANK
OFF_KPE = OFF_CKV + B_KV_RANK
IN_PAD_WIDTH = OFF_KPE + LANES

FFN_CHUNK = 256
ROW_TILE = 512
OUT_ROW_TILE = 1024
Q_TILE = 1024
Q_SUB = 512
KEY_CHUNK = 512


def _layer_norm(z, g, b):
    mu = jnp.mean(z, axis=-1, keepdims=True)
    zc = z - mu
    var = jnp.mean(zc * zc, axis=-1, keepdims=True)
    return zc * lax.rsqrt(var + LN_EPS) * g + b


def _swiglu_chunk(xb, w1_ref, w3_ref, w2_ref, c):
    cols = slice(c * FFN_CHUNK, (c + 1) * FFN_CHUNK)
    u = jnp.dot(xb, w1_ref[:, cols], preferred_element_type=F32)
    g = jnp.dot(xb, w3_ref[:, cols], preferred_element_type=F32)
    a = (u * jax.nn.sigmoid(u) * g).astype(BF16)
    return jnp.dot(a, w2_ref[cols, :], preferred_element_type=F32)


def _swiglu(xb, w1_ref, w3_ref, w2_ref):
    acc = _swiglu_chunk(xb, w1_ref, w3_ref, w2_ref, 0)
    for c in range(1, w1_ref.shape[1] // FFN_CHUNK):
        acc = acc + _swiglu_chunk(xb, w1_ref, w3_ref, w2_ref, c)
    return acc


def _rot_b(x, lane):
    half = B_ROPE_DIM // 2
    return jnp.where(lane < B_NOPE_DIM + half, pltpu.roll(x, LANES - half, 1), pltpu.roll(x, half, 1))


def _ffn_proj_kernel(x_ref, w1_ref, w3_ref, w2_ref, lng_ref, lnb_ref, win_ref,
                     gq_ref, gk_ref, gcq_ref, gckv_ref, wuq_ref, wuqr_ref, wuk_ref, wuv_ref,
                     ca_ref, sa_ref, cb_ref, sb_ref,
                     h1_ref, qa_ref, ka_ref, vat_ref, qb_ref, kb_ref, vbt_ref):
    tm = x_ref.shape[0]
    x = x_ref[...]
    y = _swiglu(x.astype(BF16), w1_ref, w3_ref, w2_ref)
    h1 = _layer_norm(ALPHA * x + 0.5 * y, lng_ref[...], lnb_ref[...])
    h1_ref[...] = h1

    proj = jnp.dot(h1.astype(BF16), win_ref[...], preferred_element_type=F32)

    n_pos_blocks = ca_ref.shape[0] // tm
    row0 = pl.multiple_of((pl.program_id(0) % n_pos_blocks) * tm, tm)
    ca = ca_ref[pl.ds(row0, tm), :]
    sa = sa_ref[pl.ds(row0, tm), :]
    cb = cb_ref[pl.ds(row0, tm), :]
    sb = sb_ref[pl.ds(row0, tm), :]

    lane = lax.broadcasted_iota(jnp.int32, (tm, LANES), 1)
    first = (lane % A_HEAD_DIM) < A_HEAD_DIM // 2

    def norm_rope_a(xc, g):
        x2 = xc * xc
        s0 = jnp.sum(jnp.where(first, x2, 0.0), axis=-1, keepdims=True)
        s1 = jnp.sum(jnp.where(first, 0.0, x2), axis=-1, keepdims=True)
        ms = jnp.where(first, s0, s1) * (1.0 / A_HEAD_DIM)
        n = xc * lax.rsqrt(ms + RMS_EPS) * g
        return n * ca + pltpu.roll(n, LANES // 2, 1) * sa

    gq = gq_ref[...]
    for c in range(A_WIDTH // LANES):
        qa_ref[:, c * LANES:(c + 1) * LANES] = norm_rope_a(
            proj[:, OFF_QA + c * LANES:OFF_QA + (c + 1) * LANES], gq).astype(BF16)
    ka_ref[...] = norm_rope_a(proj[:, OFF_KA:OFF_KA + LANES], gk_ref[...]).astype(BF16)
    vat_ref[...] = proj[:, OFF_VA:OFF_VA + LANES].T.astype(BF16)

    def rms(v, g):
        return v * lax.rsqrt(jnp.mean(v * v, axis=-1, keepdims=True) + RMS_EPS) * g

    cqn = rms(proj[:, OFF_CQ:OFF_CQ + B_Q_RANK], gcq_ref[...]).astype(BF16)
    ckvn = rms(proj[:, OFF_CKV:OFF_CKV + B_KV_RANK], gckv_ref[...]).astype(BF16)
    qb = jnp.dot(cqn, wuq_ref[...], preferred_element_type=F32)
    qb_rot = jnp.dot(cqn, wuqr_ref[...], preferred_element_type=F32)
    kn = jnp.dot(ckvn, wuk_ref[...], preferred_element_type=F32)
    vbt_ref[...] = jnp.dot(ckvn, wuv_ref[...], preferred_element_type=F32).T.astype(BF16)

    kpe = proj[:, OFF_KPE:OFF_KPE + LANES]
    kpe = kpe * cb + _rot_b(kpe, lane) * sb
    q_scale = B_QK_DIM ** -0.5 * LOG2_E
    for h in range(B_HEADS):
        xh = qb[:, h * LANES:(h + 1) * LANES]
        qb_ref[:, h * LANES:(h + 1) * LANES] = (
            (xh * cb + qb_rot[:, h * LANES:(h + 1) * LANES] * sb) * q_scale).astype(BF16)
        kb_ref[:, h * LANES:(h + 1) * LANES] = (kn[:, h * LANES:(h + 1) * LANES] + kpe).astype(BF16)


def _attention_kernel(qa_ref, ka_ref, vat_ref, qb_ref, kb_ref, vbt_ref,
                      kam_ref, vamt_ref, kbm_ref, vbmt_ref, ga_ref, gb_ref, o_ref, s0_ref, s1_ref):
    tq = Q_SUB
    n_sub = qa_ref.shape[0] // tq
    n_keys = ka_ref.shape[0]
    n_chunks = n_keys // KEY_CHUNK
    hd = A_HEAD_DIM
    s_refs = (s0_ref, s1_ref)
    lane = lax.broadcasted_iota(jnp.int32, (tq, LANES), 1)
    first = (lane % A_HEAD_DIM) < A_HEAD_DIM // 2
    ones = jnp.ones((BF16_SUBLANES, KEY_CHUNK), BF16)
    zero = jnp.minimum(pl.program_id(1), 0)
    meta_rows = pl.ds(pl.multiple_of(zero + n_keys, N_META), N_META)

    def chunk_rows(j):
        return pl.ds(pl.multiple_of(zero + j * KEY_CHUNK, KEY_CHUNK), KEY_CHUNK)

    def q_a(h, t):
        c, g = h // 2, h // A_GROUP
        half = A_HEAD_DIM // 2
        qc = qa_ref[t * tq:(t + 1) * tq, c * LANES:(c + 1) * LANES].astype(F32)
        if (h % 2) == g:
            on_kv_lanes = qc
        else:
            on_kv_lanes = pltpu.roll(qc, half if g == 1 else LANES - half, 1)
        return jnp.where(first if g == 0 else jnp.logical_not(first), on_kv_lanes, 0.0)

    heads = []
    for h in range(A_HEADS):
        g = h // A_GROUP
        heads.append((lambda t, h=h: q_a(h, t), lambda: ka_ref[...],
                      lambda g=g: vat_ref[g * hd:(g + 1) * hd, :], lambda: kam_ref[...],
                      lambda g=g: vamt_ref[g * hd:(g + 1) * hd, :]))
    for h in range(B_HEADS):
        heads.append((lambda t, h=h: qb_ref[t * tq:(t + 1) * tq, h * LANES:(h + 1) * LANES].astype(F32),
                      lambda h=h: kb_ref[:, h * LANES:(h + 1) * LANES],
                      lambda h=h: vbt_ref[h * hd:(h + 1) * hd, :],
                      lambda h=h: kbm_ref[:, h * LANES:(h + 1) * LANES],
                      lambda h=h: vbmt_ref[h * hd:(h + 1) * hd, :]))

    items = [(t, h) for t in range(n_sub) for h in range(len(heads))]

    def scores_begin(n):
        t, h = items[n]
        qt = heads[h][0](t).T.astype(BF16)
        smt = jnp.dot(heads[h][3](), qt, preferred_element_type=F32)
        s_refs[n % 2][meta_rows, :] = smt
        return qt, jnp.max(smt, axis=0, keepdims=True)

    def scores_chunk(n, j, qt, m):
        k = heads[items[n][1]][1]()
        st = jnp.dot(k[j * KEY_CHUNK:(j + 1) * KEY_CHUNK, :], qt, preferred_element_type=F32)
        s_refs[n % 2][chunk_rows(j), :] = st
        return jnp.maximum(m, jnp.max(st, axis=0, keepdims=True))

    def values_begin(n, m):
        pm = jnp.exp2((s_refs[n % 2][meta_rows, :] - m).astype(BF16))
        lhs = jnp.concatenate([heads[items[n][1]][4](), ones[:, :N_META]], axis=0)
        return jnp.dot(lhs, pm, preferred_element_type=F32)

    def values_chunk(n, j, m, ot):
        p = jnp.exp2((s_refs[n % 2][chunk_rows(j), :] - m).astype(BF16))
        lhs = jnp.concatenate([heads[items[n][1]][2]()[:, j * KEY_CHUNK:(j + 1) * KEY_CHUNK], ones], axis=0)
        return ot + jnp.dot(lhs, p, preferred_element_type=F32)

    def group_rms(head_rows, g):
        xt = jnp.concatenate(head_rows, axis=0)
        r = lax.rsqrt(jnp.mean(xt * xt, axis=0, keepdims=True) + RMS_EPS)
        return (xt * r * g).T.astype(BF16)

    outs = []
    qt, m_next = scores_begin(0)
    for j in range(n_chunks):
        m_next = scores_chunk(0, j, qt, m_next)
    for n in range(len(items)):
        m = m_next
        more = n + 1 < len(items)
        if more:
            qt, m_next = scores_begin(n + 1)
        ot = values_begin(n, m)
        for j in range(n_chunks):
            if more:
                m_next = scores_chunk(n + 1, j, qt, m_next)
            ot = values_chunk(n, j, m, ot)
        outs.append(ot[:hd] * (1.0 / ot[hd:hd + 1]))
        if len(outs) == len(heads):
            rows = slice(items[n][0] * tq, (items[n][0] + 1) * tq)
            o_ref[rows, :A_WIDTH] = group_rms(outs[:A_HEADS], ga_ref[...])
            o_ref[rows, A_WIDTH:] = group_rms(outs[A_HEADS:], gb_ref[...])
            outs = []


def _out_ffn_kernel(o_ref, h1_ref, wout_ref, g2_ref, b2_ref, w1_ref, w3_ref, w2_ref, g3_ref, b3_ref,
                    out_ref):
    half = o_ref.shape[0] // 2
    ra, rb = slice(0, half), slice(half, 2 * half)
    n_chunks = w1_ref.shape[1] // FFN_CHUNK
    ffn = (w1_ref, w3_ref, w2_ref)
    ln2 = lambda rows, mix: _layer_norm(ALPHA * h1_ref[rows, :] + mix, g2_ref[...], b2_ref[...])
    ln3 = lambda h2, y: _layer_norm(ALPHA * h2 + 0.5 * y, g3_ref[...], b3_ref[...])

    mix_a = jnp.dot(o_ref[ra, :], wout_ref[...], preferred_element_type=F32)
    mix_b = jnp.dot(o_ref[rb, :], wout_ref[...], preferred_element_type=F32)
    h2a = ln2(ra, mix_a)
    xa = h2a.astype(BF16)
    ya = _swiglu_chunk(xa, *ffn, 0)
    h2b = ln2(rb, mix_b)
    xb = h2b.astype(BF16)
    for c in range(1, n_chunks):
        ya = ya + _swiglu_chunk(xa, *ffn, c)
    yb = _swiglu_chunk(xb, *ffn, 0)
    out_ref[ra, :] = ln3(h2a, ya)
    for c in range(1, n_chunks):
        yb = yb + _swiglu_chunk(xb, *ffn, c)
    out_ref[rb, :] = ln3(h2b, yb)


def _resident(shape):
    return pl.BlockSpec(shape, lambda *_: (0,) * len(shape), pipeline_mode=pl.Buffered(1))


def _ffn_proj_call(x2d, seq, tm, ffn_w, ln, win, gains, wu, tables):
    rows, d = x2d.shape
    n_pos_blocks = seq // tm
    row = lambda w: pl.BlockSpec((tm, w), lambda i: (i, 0))
    col = lambda w: pl.BlockSpec((w, tm), lambda i: (i // n_pos_blocks, i % n_pos_blocks))
    resident = [*ffn_w, *ln, win, *gains, *wu, *tables]
    tok = lambda w, dt=BF16: jax.ShapeDtypeStruct((rows, w), dt)
    feat = lambda w: jax.ShapeDtypeStruct((rows // seq * w, seq), BF16)
    return pl.pallas_call(
        _ffn_proj_kernel,
        grid=(rows // tm,),
        in_specs=[row(d)] + [_resident(a.shape) for a in resident],
        out_specs=[row(d), row(A_WIDTH), row(LANES), col(LANES), row(B_HEADS * LANES), row(B_HEADS * LANES),
                   col(B_WIDTH)],
        out_shape=[tok(d, F32), tok(A_WIDTH), tok(LANES), feat(LANES), tok(B_HEADS * LANES), tok(B_HEADS * LANES),
                   feat(B_WIDTH)],
        compiler_params=pltpu.CompilerParams(dimension_semantics=("parallel",), vmem_limit_bytes=VMEM_LIMIT_BYTES),
        name="ffn_proj",
    )(x2d, *resident)


def _attention_call(bsz, seq, tq, qa, ka, vat, qb, kb, vbt, meta_kv, ga, gb):
    nq = seq // tq
    qspec = lambda w: pl.BlockSpec((tq, w), lambda b, j: (b * nq + j, 0))
    kspec = lambda w: pl.BlockSpec((seq, w), lambda b, j: (b, 0))
    vspec = lambda w: pl.BlockSpec((w, seq), lambda b, j: (b, 0))
    small = [*meta_kv, ga, gb]
    return pl.pallas_call(
        _attention_kernel,
        grid=(bsz, nq),
        in_specs=[qspec(qa.shape[1]), kspec(ka.shape[1]), vspec(LANES),
                  qspec(qb.shape[1]), kspec(kb.shape[1]), vspec(B_WIDTH)]
                 + [pl.BlockSpec(a.shape, lambda b, j: (0, 0)) for a in small],
        out_specs=qspec(A_WIDTH + B_WIDTH),
        out_shape=jax.ShapeDtypeStruct((bsz * seq, A_WIDTH + B_WIDTH), BF16),
        scratch_shapes=[pltpu.VMEM((seq + N_META, Q_SUB), F32) for _ in range(2)],
        compiler_params=pltpu.CompilerParams(dimension_semantics=("parallel", "arbitrary"),
                                             vmem_limit_bytes=VMEM_LIMIT_BYTES),
        name="attention",
    )(qa, ka, vat, qb, kb, vbt, *small)


def _out_ffn_call(o, h1, tm, wout, ln2, ffn_w, ln3):
    rows, d = h1.shape
    row = lambda w: pl.BlockSpec((tm, w), lambda i: (i, 0))
    resident = [wout, *ln2, *ffn_w, *ln3]
    return pl.pallas_call(
        _out_ffn_kernel,
        grid=(rows // tm,),
        in_specs=[row(o.shape[1]), row(d)] + [_resident(a.shape) for a in resident],
        out_specs=row(d),
        out_shape=jax.ShapeDtypeStruct((rows, d), F32),
        compiler_params=pltpu.CompilerParams(dimension_semantics=("parallel",), vmem_limit_bytes=VMEM_LIMIT_BYTES),
        name="out_ffn",
    )(o, h1, *resident)


def _ffn_weights(w1, w3, w2):
    return w1.astype(BF16), w3.astype(BF16), w2.astype(BF16)


def _rope_tables(seq):
    rows = jnp.repeat(jnp.arange(seq // GRID_W, dtype=jnp.int32), GRID_W).astype(F32)[:, None]
    cols = jnp.tile(jnp.arange(GRID_W, dtype=jnp.int32), seq // GRID_W).astype(F32)[:, None]

    def cos_sin(rot_dim):
        axis_dim = rot_dim // 2
        inv_freq = ROPE_THETA ** (-jnp.arange(0, axis_dim, 2, dtype=F32) / axis_dim)
        ang = jnp.concatenate([rows * inv_freq, cols * inv_freq], axis=-1)
        return jnp.cos(ang), jnp.sin(ang)

    cos_a, sin_a = cos_sin(A_HEAD_DIM)
    ca = jnp.tile(cos_a, (1, LANES // cos_a.shape[1]))
    sa = jnp.concatenate([-sin_a, -sin_a, sin_a, sin_a], axis=-1)
    cos_b, sin_b = cos_sin(B_ROPE_DIM)
    pad = LANES - B_QK_DIM
    cb = jnp.concatenate([jnp.ones((seq, B_NOPE_DIM), F32), cos_b, cos_b, jnp.ones((seq, pad), F32)], axis=-1)
    sb = jnp.concatenate([jnp.zeros((seq, B_NOPE_DIM), F32), -sin_b, sin_b, jnp.zeros((seq, pad), F32)], axis=-1)
    return ca, sa, cb, sb


def kernel(x, meta_tokens, ffn1_w1, ffn1_w3, ffn1_w2, ln1_g, ln1_b, w_in, q_norm_a, k_norm_a, cq_norm, ckv_norm, w_uq, w_ukv, out_norm_a, out_norm_b, w_out, ln2_g, ln2_b, ffn2_w1, ffn2_w3, ffn2_w2, ln3_g, ln3_b):
    bsz, seq, d = x.shape
    assert seq % ROW_TILE == 0 and (bsz * seq) % OUT_ROW_TILE == 0 and seq % Q_TILE == 0 and seq % GRID_W == 0 and seq % KEY_CHUNK == 0
    assert ffn1_w1.shape[2] % FFN_CHUNK == 0 and d % LANES == 0 and Q_TILE % Q_SUB == 0
    i = 0

    ffn1 = _ffn_weights(ffn1_w1[i], ffn1_w3[i], ffn1_w2[i])
    ffn2 = _ffn_weights(ffn2_w1[i], ffn2_w3[i], ffn2_w2[i])
    row_vec = lambda v: v.reshape(1, -1).astype(F32)
    col_vec = lambda v: v.reshape(-1, 1).astype(F32)
    half = A_HEAD_DIM // 2
    pair_perm = jnp.arange(LANES).reshape(2, 2, half).transpose(1, 0, 2).reshape(LANES)
    qk_perm = (jnp.arange(OFF_VA).reshape(-1, LANES)[:, pair_perm]).reshape(-1)
    win = w_in[i]
    win = jnp.concatenate([win[:, qk_perm], win[:, OFF_VA:]], axis=1)
    n_main = OFF_KPE
    kpe_cols = jnp.pad(win[:, n_main:], ((0, 0), (B_NOPE_DIM, LANES - B_QK_DIM)))
    win_p = jnp.concatenate([win[:, :n_main], kpe_cols], axis=1).astype(BF16)
    wuq_p = jnp.pad(w_uq[i].reshape(B_Q_RANK, B_HEADS, B_QK_DIM),
                    ((0, 0), (0, 0), (0, LANES - B_QK_DIM))).reshape(B_Q_RANK, B_HEADS * LANES).astype(BF16)
    rope_lo, rope_mid = B_NOPE_DIM, B_NOPE_DIM + B_ROPE_DIM // 2
    wuq3 = wuq_p.reshape(B_Q_RANK, B_HEADS, LANES)
    zero_cols = lambda n: jnp.zeros((B_Q_RANK, B_HEADS, n), BF16)
    wuqr_p = jnp.concatenate([zero_cols(rope_lo), wuq3[:, :, rope_mid:B_QK_DIM], wuq3[:, :, rope_lo:rope_mid],
                              zero_cols(LANES - B_QK_DIM)], axis=2).reshape(B_Q_RANK, B_HEADS * LANES)
    wukv = w_ukv[i].reshape(B_KV_RANK, B_HEADS, B_NOPE_DIM + B_V_DIM)
    wuk_p = jnp.pad(wukv[:, :, :B_NOPE_DIM],
                    ((0, 0), (0, 0), (0, LANES - B_NOPE_DIM))).reshape(B_KV_RANK, B_HEADS * LANES).astype(BF16)
    wuv = wukv[:, :, B_NOPE_DIM:].reshape(B_KV_RANK, B_WIDTH).astype(BF16)
    pair_gain = lambda g: jnp.tile(g, LANES // A_HEAD_DIM)[pair_perm].reshape(1, LANES).astype(F32)
    gq = pair_gain(q_norm_a[i] * (A_HEAD_DIM ** -0.5 * LOG2_E))
    gk = pair_gain(k_norm_a[i])
    gains = (gq, gk, row_vec(cq_norm[i]), row_vec(ckv_norm[i]))
    ln1 = (row_vec(ln1_g[i]), row_vec(ln1_b[i]))
    wu = (wuq_p, wuqr_p, wuk_p, wuv)

    tables = _rope_tables(seq)
    ones, zeros = jnp.ones((N_META, LANES), F32), jnp.zeros((N_META, LANES), F32)
    meta_tables = (ones, zeros, ones, zeros)

    x2d = x.reshape(bsz * seq, d)
    h1, qa, ka, vat, qb, kb, vbt = _ffn_proj_call(x2d, seq, ROW_TILE, ffn1, ln1, win_p, gains, wu, tables)
    _, _, kam, vamt, _, kbm, vbmt = _ffn_proj_call(meta_tokens.astype(F32), N_META, N_META, ffn1, ln1, win_p, gains,
                                                   wu, meta_tables)

    o = _attention_call(bsz, seq, Q_TILE, qa, ka, vat, qb, kb, vbt, (kam, vamt, kbm, vbmt),
                        col_vec(out_norm_a[i]), col_vec(out_norm_b[i]))

    out = _out_ffn_call(o, h1, OUT_ROW_TILE, w_out[i].astype(BF16), (row_vec(ln2_g[i]), row_vec(ln2_b[i])),
                        ffn2, (row_vec(ln3_g[i]), row_vec(ln3_b[i])))
    return out.reshape(bsz, seq, d)
```
